```python
import math
import jax, jax.numpy as jnp
from jax import lax
import numpy as np

D_MODEL = 4096
BATCH = 4
SEQ = 4096
DEPTH = 4

HEAD_DIM = 128
N_HEADS = D_MODEL // HEAD_DIM
N_RET_HEADS = (N_HEADS * 3) // 8
N_SB_HEADS = (N_HEADS - N_RET_HEADS) // 2
N_DIFF_HEADS = N_HEADS - N_RET_HEADS - N_SB_HEADS
RET_W = N_RET_HEADS * HEAD_DIM
SB_W = N_SB_HEADS * HEAD_DIM
DIFF_W = N_DIFF_HEADS * HEAD_DIM
MIX_W = RET_W + SB_W + DIFF_W
DIFF_QK_DIM = HEAD_DIM // 2
BLOCK = 128
DEEPNORM_ALPHA = (2.0 * DEPTH) ** 0.25
DEEPNORM_BETA = (8.0 * DEPTH) ** -0.25
LN_EPS = 1e-5

kernel_name = "hybrid_ret_sb_diff_deepnorm"


def _split_points():
    widths = [RET_W] * 4 + [SB_W] * 4 + [DIFF_W] * 4
    pts, acc = [], 0
    for w in widths[:-1]:
        acc += w
        pts.append(acc)
    return pts


def _alibi_slopes(n):
    def pow2(m):
        start = 2.0 ** (-8.0 / m)
        return [start ** (i + 1) for i in range(m)]
    if math.log2(n).is_integer():
        s = pow2(n)
    else:
        c = 2 ** math.floor(math.log2(n))
        s = pow2(c) + pow2(2 * c)[0::2][: n - c]
    return jnp.asarray(np.asarray(s, dtype=np.float32))


def _retention_log_decay(n):
    g = 1.0 - 2.0 ** (-5.0 - np.arange(n))
    return jnp.asarray(np.log(g).astype(np.float32))


def layer_norm(x, g, b):
    xf = x.astype(jnp.float32)
    mu = jnp.mean(xf, axis=-1, keepdims=True)
    xc = xf - mu
    var = jnp.mean(xc * xc, axis=-1, keepdims=True)
    return (xc * lax.rsqrt(var + LN_EPS)).astype(x.dtype) * g + b


def head_layer_norm(o):
    of = o.astype(jnp.float32)
    mu = jnp.mean(of, axis=-1, keepdims=True)
    oc = of - mu
    var = jnp.mean(oc * oc, axis=-1, keepdims=True)
    return oc * lax.rsqrt(var + LN_EPS)


def head_rms_norm(o):
    of = o.astype(jnp.float32)
    return of * lax.rsqrt(jnp.mean(of * of, axis=-1, keepdims=True) + LN_EPS)


def retention_chunkwise(q, k, v, log_gamma):
    B, S, H, Dh = q.shape
    n = S // BLOCK
    to_chunks = lambda a: a.reshape(B, n, BLOCK, H, Dh).transpose(1, 0, 3, 2, 4)
    qc, kc, vc = to_chunks(q), to_chunks(k), to_chunks(v)
    idx = jnp.arange(BLOCK, dtype=jnp.float32)
    rel = idx[:, None] - idx[None, :]
    causal = rel >= 0
    inner_decay = jnp.where(causal, jnp.exp(log_gamma[:, None, None] * jnp.where(causal, rel, 0.0)), 0.0)
    q_decay = jnp.exp(log_gamma[:, None] * (idx + 1.0))[None, :, :, None]
    k_decay = jnp.exp(log_gamma[:, None] * (BLOCK - 1.0 - idx))[None, :, :, None]
    chunk_decay = jnp.exp(log_gamma * BLOCK)[None, :, None, None]

    def step(state, inp):
        qb, kb, vb = inp
        scores = jnp.einsum('bhqd,bhkd->bhqk', qb, kb) * inner_decay
        inner = jnp.einsum('bhqk,bhkd->bhqd', scores, vb)
        cross = jnp.einsum('bhqd,bhde->bhqe', qb, state) * q_decay
        new_state = state * chunk_decay + jnp.einsum('bhkd,bhke->bhde', kb * k_decay, vb)
        return new_state, inner + cross

    state0 = jnp.zeros((B, H, Dh, Dh), jnp.float32)
    _, out = lax.scan(step, state0, (qc, kc, vc))
    return out.transpose(1, 0, 3, 2, 4).reshape(B, S, H, Dh)


def stick_breaking_attention(q, k, v):
    S, Dh = q.shape[2], q.shape[3]
    scale = Dh ** -0.5
    outs = []
    for blk in range(S // BLOCK):
        q0, kend = blk * BLOCK, (blk + 1) * BLOCK
        qb, kb, vb = q[:, :, q0:kend], k[:, :, :kend], v[:, :, :kend]
        z = jnp.einsum('bhqd,bhkd->bhqk', qb, kb).astype(jnp.float32) * scale
        t = q0 + jnp.arange(BLOCK)
        s = jnp.arange(kend)
        mask = s[None, :] < t[:, None]
        log_1mb = jnp.where(mask, jax.nn.log_sigmoid(-z), 0.0)
        cum = jnp.cumsum(log_1mb, axis=-1)
        log_a = jax.nn.log_sigmoid(z) + (cum[..., -1:] - cum)
        a = jnp.where(mask, jnp.exp(log_a), 0.0)
        outs.append(jnp.einsum('bhqk,bhkd->bhqd', a.astype(vb.dtype), vb))
    return jnp.concatenate(outs, axis=2)


def differential_attention(q1, q2, k1, k2, v, lam, slopes):
    S = q1.shape[2]
    scale = q1.shape[3] ** -0.5
    outs = []
    for blk in range(S // BLOCK):
        q0, kend = blk * BLOCK, (blk + 1) * BLOCK
        t = q0 + jnp.arange(BLOCK)
        s = jnp.arange(kend)
        rel = (t[:, None] - s[None, :]).astype(jnp.float32)
        causal = rel >= 0
        bias = -slopes[:, None, None] * rel

        def probs(qx, kx):
            z = jnp.einsum('bhqd,bhkd->bhqk', qx[:, :, q0:kend], kx[:, :, :kend]).astype(jnp.float32) * scale + bias
            return jax.nn.softmax(jnp.where(causal, z, -jnp.inf), axis=-1)

        a = probs(q1, k1) - lam * probs(q2, k2)
        vb = v[:, :, :kend]
        outs.append(jnp.einsum('bhqk,bhkd->bhqd', a.astype(vb.dtype), vb))
    return jnp.concatenate(outs, axis=2)


def setup_inputs(seed: int = 0) -> dict:
    key = jax.random.key(seed)
    ks = jax.random.split(key, 12)
    f32 = jnp.float32
    x = jax.random.normal(ks[0], (BATCH, SEQ, D_MODEL), f32)
    w_in = jax.random.normal(ks[1], (DEPTH, D_MODEL, 4 * MIX_W), f32) * (D_MODEL ** -0.5)
    w_out = jax.random.normal(ks[2], (DEPTH, MIX_W, D_MODEL), f32) * ((MIX_W ** -0.5) * DEEPNORM_BETA)
    ln_g = 1.0 + 0.02 * jax.random.normal(ks[3], (DEPTH, D_MODEL), f32)
    ln_b = 0.02 * jax.random.normal(ks[4], (DEPTH, D_MODEL), f32)
    ret_gn_g = 1.0 + 0.02 * jax.random.normal(ks[5], (DEPTH, RET_W), f32)
    diff_ln_g = 1.0 + 0.02 * jax.random.normal(ks[6], (DEPTH, DIFF_W), f32)
    lam_q1 = 0.1 * jax.random.normal(ks[7], (DEPTH, DIFF_QK_DIM), f32)
    lam_k1 = 0.1 * jax.random.normal(ks[8], (DEPTH, DIFF_QK_DIM), f32)
    lam_q2 = 0.1 * jax.random.normal(ks[9], (DEPTH, DIFF_QK_DIM), f32)
    lam_k2 = 0.1 * jax.random.normal(ks[10], (DEPTH, DIFF_QK_DIM), f32)
    return {"x": x, "w_in": w_in, "w_out": w_out, "ln_g": ln_g, "ln_b": ln_b,
            "ret_gn_g": ret_gn_g, "diff_ln_g": diff_ln_g,
            "lam_q1": lam_q1, "lam_k1": lam_k1, "lam_q2": lam_q2, "lam_k2": lam_k2}


def reference(x, w_in, w_out, ln_g, ln_b, ret_gn_g, diff_ln_g, lam_q1, lam_k1, lam_q2, lam_k2):
    B, S, _ = x.shape
    log_gamma = _retention_log_decay(N_RET_HEADS)
    slopes = _alibi_slopes(N_DIFF_HEADS)
    split_pts = _split_points()
    to_heads = lambda a, h: a.reshape(B, S, h, HEAD_DIM)
    bhsd = lambda a: a.transpose(0, 2, 1, 3)

    for layer in range(DEPTH):
        proj = jnp.einsum('bsd,de->bse', x, w_in[layer])
        (r_q, r_k, r_v, r_g, s_q, s_k, s_v, s_g, d_q, d_k, d_v, d_g) = jnp.split(proj, split_pts, axis=-1)

        rq = to_heads(r_q, N_RET_HEADS)
        rk = to_heads(r_k, N_RET_HEADS) * (HEAD_DIM ** -0.5)
        rv = to_heads(r_v, N_RET_HEADS)
        ro = retention_chunkwise(rq, rk, rv, log_gamma)
        ro = head_layer_norm(ro).reshape(B, S, RET_W).astype(x.dtype) * ret_gn_g[layer]
        ro = ro * jax.nn.silu(r_g)

        so = stick_breaking_attention(bhsd(to_heads(s_q, N_SB_HEADS)), bhsd(to_heads(s_k, N_SB_HEADS)),
                                      bhsd(to_heads(s_v, N_SB_HEADS)))
        so = so.transpose(0, 2, 1, 3).reshape(B, S, SB_W) * jax.nn.silu(s_g)

        lambda_init = 0.8 - 0.6 * math.exp(-0.3 * layer)
        lam = (jnp.exp(jnp.sum(lam_q1[layer].astype(jnp.float32) * lam_k1[layer].astype(jnp.float32)))
               - jnp.exp(jnp.sum(lam_q2[layer].astype(jnp.float32) * lam_k2[layer].astype(jnp.float32)))
               + lambda_init)
        dq = bhsd(to_heads(d_q, N_DIFF_HEADS))
        dk = bhsd(to_heads(d_k, N_DIFF_HEADS))
        dv = bhsd(to_heads(d_v, N_DIFF_HEADS))
        do = differential_attention(dq[..., :DIFF_QK_DIM], dq[..., DIFF_QK_DIM:],
                                    dk[..., :DIFF_QK_DIM], dk[..., DIFF_QK_DIM:], dv, lam, slopes)
        do = head_rms_norm(do.transpose(0, 2, 1, 3)).reshape(B, S, DIFF_W).astype(x.dtype)
        do = do * diff_ln_g[layer] * (1.0 - lambda_init) * jax.nn.silu(d_g)

        mixed = jnp.concatenate([ro, so, do.astype(ro.dtype)], axis=-1)
        y = jnp.einsum('bse,ed->bsd', mixed, w_out[layer]).astype(x.dtype)
        x = layer_norm(DEEPNORM_ALPHA * x + y, ln_g[layer], ln_b[layer])
    return x
```

```python
import functools
import math

import jax
import jax.numpy as jnp
import numpy as np
from jax import lax
from jax.experimental import pallas as pl
from jax.experimental.pallas import tpu as pltpu

HEAD_DIM = 128
N_RET = 12
N_SB = 10
N_DIFF = 10
RET_W = N_RET * HEAD_DIM
SB_W = N_SB * HEAD_DIM
DIFF_W = N_DIFF * HEAD_DIM
CHUNK = 128
LN_EPS = 1e-5

RET_Q, RET_K, RET_V, RET_G = 0, N_RET, 2 * N_RET, 3 * N_RET
SB_BASE = 4 * N_RET
SB_Q, SB_K, SB_V, SB_G = SB_BASE, SB_BASE + N_SB, SB_BASE + 2 * N_SB, SB_BASE + 3 * N_SB
DIFF_BASE = SB_BASE + 4 * N_SB
DIFF_Q, DIFF_K, DIFF_V, DIFF_G = (DIFF_BASE, DIFF_BASE + N_DIFF, DIFF_BASE + 2 * N_DIFF,
                                  DIFF_BASE + 3 * N_DIFF)

V7X_VMEM_LIMIT = 56 * 1024 * 1024

F32 = jnp.float32
BF16 = jnp.bfloat16


def _alibi_slopes(n):
    def pow2(m):
        start = 2.0 ** (-8.0 / m)
        return [start ** (i + 1) for i in range(m)]
    if math.log2(n).is_integer():
        s = pow2(n)
    else:
        c = 2 ** math.floor(math.log2(n))
        s = pow2(c) + pow2(2 * c)[0::2][: n - c]
    return np.asarray(s, dtype=np.float32)


def _retention_log_decay(n):
    g = 1.0 - 2.0 ** (-5.0 - np.arange(n))
    return np.log(g).astype(np.float32)


def _silu(g):
    return g * (1.0 / (1.0 + jnp.exp(-g)))


def _dot_nt(a, b):
    return lax.dot_general(a, b, (((1,), (1,)), ((), ())), preferred_element_type=F32)


def _dot(a, b):
    return jnp.dot(a, b, preferred_element_type=F32)


def _matmul_kernel(x_ref, w_ref, o_ref):
    o_ref[...] = _dot(x_ref[...], w_ref[...]).astype(o_ref.dtype)


def _proj(xb, wb, *, tm=1024, tn=1024):
    m, k = xb.shape
    n = wb.shape[1]
    return pl.pallas_call(
        _matmul_kernel,
        grid=(m // tm, n // tn),
        in_specs=[pl.BlockSpec((tm, k), lambda i, j: (i, 0)),
                  pl.BlockSpec((k, tn), lambda i, j: (0, j))],
        out_specs=pl.BlockSpec((tm, tn), lambda i, j: (i, j)),
        out_shape=jax.ShapeDtypeStruct((m, n), BF16),
        compiler_params=pltpu.CompilerParams(
            dimension_semantics=("parallel", "parallel"), vmem_limit_bytes=V7X_VMEM_LIMIT),
        name="proj_matmul",
    )(xb, wb)


def _ret_kernel(lg_ref, q_ref, k_ref, v_ref, g_ref, gn_ref, o_ref, state_ref):
    h = pl.program_id(1)
    lg = lg_ref[h]
    seq = q_ref.shape[0]
    c = CHUNK
    scale = HEAD_DIM ** -0.5
    row = lax.broadcasted_iota(jnp.int32, (c, c), 0)
    col = lax.broadcasted_iota(jnp.int32, (c, c), 1)
    rel = (row - col).astype(F32)
    causal = row >= col
    inner = jnp.where(causal, jnp.exp(lg * jnp.where(causal, rel, 0.0)), 0.0) * scale
    rowf = row.astype(F32)
    q_decay = jnp.exp(lg * (rowf + 1.0))
    k_decay = jnp.exp(lg * (c - 1.0 - rowf)) * scale
    chunk_decay = jnp.exp(lg * c)
    gn = gn_ref[...]

    state_ref[...] = jnp.zeros_like(state_ref)

    def body(ci, carry):
        r0 = pl.multiple_of(ci * c, c)
        q = q_ref[pl.ds(r0, c), :]
        k = k_ref[pl.ds(r0, c), :]
        v = v_ref[pl.ds(r0, c), :]
        g = g_ref[pl.ds(r0, c), :].astype(F32)
        state = state_ref[...]
        scores = _dot_nt(q, k) * inner
        o = _dot(scores.astype(BF16), v)
        o = o + _dot(q, state.astype(BF16)) * q_decay
        kd = (k.astype(F32) * k_decay).astype(BF16)
        upd = lax.dot_general(kd, v, (((0,), (0,)), ((), ())), preferred_element_type=F32)
        state_ref[...] = state * chunk_decay + upd
        mu = jnp.mean(o, axis=-1, keepdims=True)
        oc = o - mu
        var = jnp.mean(oc * oc, axis=-1, keepdims=True)
        y = oc * lax.rsqrt(var + LN_EPS) * gn * _silu(g)
        o_ref[pl.ds(r0, c), :] = y.astype(o_ref.dtype)
        return carry

    lax.fori_loop(0, seq // c, body, 0)


def _retention(proj, gn_g, log_gamma, batch, seq):
    blk = lambda base: pl.BlockSpec((seq, HEAD_DIM), lambda b, h: (b, base + h))
    return pl.pallas_call(
        _ret_kernel,
        grid=(batch, N_RET),
        in_specs=[pl.BlockSpec(memory_space=pltpu.SMEM),
                  blk(RET_Q), blk(RET_K), blk(RET_V), blk(RET_G),
                  pl.BlockSpec((1, HEAD_DIM), lambda b, h: (0, h))],
        out_specs=pl.BlockSpec((seq, HEAD_DIM), lambda b, h: (b, h)),
        out_shape=jax.ShapeDtypeStruct((batch * seq, RET_W), BF16),
        scratch_shapes=[pltpu.VMEM((HEAD_DIM, HEAD_DIM), F32)],
        compiler_params=pltpu.CompilerParams(
            dimension_semantics=("parallel", "parallel"), vmem_limit_bytes=V7X_VMEM_LIMIT),
        name="retention",
    )(log_gamma, proj, proj, proj, proj, gn_g)


def _sb_kernel(q_ref, k_ref, v_ref, g_ref, o_ref, *, tile):
    seq = q_ref.shape[0]
    t = tile
    scale = HEAD_DIM ** -0.5
    row = lax.broadcasted_iota(jnp.int32, (t, t), 0)
    col = lax.broadcasted_iota(jnp.int32, (t, t), 1)
    strict = col < row
    upper = jnp.where(row > col, 1.0, 0.0).astype(BF16)

    def block(q, k0, carry, acc, masked):
        k = k_ref[pl.ds(k0, t), :]
        v = v_ref[pl.ds(k0, t), :]
        z = _dot_nt(q, k) * scale
        e = jnp.exp(-jnp.abs(z))
        l = -(jnp.maximum(z, 0.0) + jnp.log(1.0 + e))
        if masked:
            l = jnp.where(strict, l, 0.0)
        hi = l.astype(BF16)
        lo = (l - hi.astype(F32)).astype(BF16)
        suffix = _dot(hi, upper) + _dot(lo, upper)
        log_a = z + l + suffix + carry
        a = jnp.exp(log_a)
        if masked:
            a = jnp.where(strict, a, 0.0)
        acc = acc + _dot(a.astype(BF16), v)
        carry = carry + jnp.sum(l, axis=-1, keepdims=True)
        return carry, acc

    def qblock(qi, _):
        q0 = pl.multiple_of(qi * t, t)
        q = q_ref[pl.ds(q0, t), :]
        carry = jnp.zeros((t, 1), F32)
        acc = jnp.zeros((t, HEAD_DIM), F32)
        carry, acc = block(q, q0, carry, acc, True)

        def kblock(jj, st):
            k0 = pl.multiple_of((qi - 1 - jj) * t, t)
            return block(q, k0, st[0], st[1], False)

        carry, acc = lax.fori_loop(0, qi, kblock, (carry, acc))
        g = g_ref[pl.ds(q0, t), :].astype(F32)
        o_ref[pl.ds(q0, t), :] = (acc * _silu(g)).astype(o_ref.dtype)
        return 0

    lax.fori_loop(0, seq // t, qblock, 0)


def _stick_breaking(proj, batch, seq, *, tile=256):
    blk = lambda base: pl.BlockSpec((seq, HEAD_DIM), lambda b, h: (b, base + h))
    return pl.pallas_call(
        functools.partial(_sb_kernel, tile=tile),
        grid=(batch, N_SB),
        in_specs=[blk(SB_Q), blk(SB_K), blk(SB_V), blk(SB_G)],
        out_specs=pl.BlockSpec((seq, HEAD_DIM), lambda b, h: (b, h)),
        out_shape=jax.ShapeDtypeStruct((batch * seq, SB_W), BF16),
        compiler_params=pltpu.CompilerParams(
            dimension_semantics=("parallel", "parallel"), vmem_limit_bytes=V7X_VMEM_LIMIT),
        name="stick_breaking",
    )(proj, proj, proj, proj)


def _diff_kernel(slope_ref, lamv_ref, q_ref, k_ref, v_ref, g_ref, gain_ref, o_ref, *, tile, out_scale,
                 lambda_init):
    h = pl.program_id(1)
    slope = slope_ref[h]
    seq = q_ref.shape[0]
    t = tile
    half = HEAD_DIM // 2
    qk_scale = half ** -0.5
    row = lax.broadcasted_iota(jnp.int32, (t, t), 0)
    col = lax.broadcasted_iota(jnp.int32, (t, t), 1)
    causal = col <= row
    base_bias = (row - col).astype(F32) * (-slope)
    lane = lax.broadcasted_iota(jnp.int32, (t, HEAD_DIM), 1)
    first_half = lane < half

    lv = lamv_ref[...]
    lam = (jnp.exp(jnp.sum(lv[0:1] * lv[1:2], axis=-1, keepdims=True))
           - jnp.exp(jnp.sum(lv[2:3] * lv[3:4], axis=-1, keepdims=True)) + lambda_init)
    gain = gain_ref[...]

    def scores(qq, k0, q0):
        k = k_ref[pl.ds(k0, t), :]
        z = _dot_nt(qq, k) * qk_scale
        bias = base_bias - slope * (q0 - k0).astype(F32)
        return z, bias

    def qblock(qi, _):
        q0 = pl.multiple_of(qi * t, t)
        q = q_ref[pl.ds(q0, t), :]
        zero = jnp.zeros_like(q)
        qq = jnp.concatenate([jnp.where(first_half, q, zero), jnp.where(first_half, zero, q)], axis=0)

        z, bias = scores(qq, q0, q0)
        z1 = jnp.where(causal, z[:t] + bias, -jnp.inf)
        z2 = jnp.where(causal, z[t:] + bias, -jnp.inf)
        m1 = jnp.max(z1, axis=-1, keepdims=True)
        m2 = jnp.max(z2, axis=-1, keepdims=True)
        p1 = jnp.exp(z1 - m1)
        p2 = jnp.exp(z2 - m2)
        l1 = jnp.sum(p1, axis=-1, keepdims=True)
        l2 = jnp.sum(p2, axis=-1, keepdims=True)
        v = v_ref[pl.ds(q0, t), :]
        acc1 = _dot(p1.astype(BF16), v)
        acc2 = _dot(p2.astype(BF16), v)

        def kblock(jj, st):
            m1, l1, acc1, m2, l2, acc2 = st
            k0 = pl.multiple_of((qi - 1 - jj) * t, t)
            z, bias = scores(qq, k0, q0)
            z1 = z[:t] + bias
            z2 = z[t:] + bias
            n1 = jnp.maximum(m1, jnp.max(z1, axis=-1, keepdims=True))
            n2 = jnp.maximum(m2, jnp.max(z2, axis=-1, keepdims=True))
            p1 = jnp.exp(z1 - n1)
            p2 = jnp.exp(z2 - n2)
            a1 = jnp.exp(m1 - n1)
            a2 = jnp.exp(m2 - n2)
            v = v_ref[pl.ds(k0, t), :]
            l1 = a1 * l1 + jnp.sum(p1, axis=-1, keepdims=True)
            l2 = a2 * l2 + jnp.sum(p2, axis=-1, keepdims=True)
            acc1 = a1 * acc1 + _dot(p1.astype(BF16), v)
            acc2 = a2 * acc2 + _dot(p2.astype(BF16), v)
            return n1, l1, acc1, n2, l2, acc2

        m1, l1, acc1, m2, l2, acc2 = lax.fori_loop(0, qi, kblock, (m1, l1, acc1, m2, l2, acc2))
        o = acc1 / l1 - lam * (acc2 / l2)
        o = o * lax.rsqrt(jnp.mean(o * o, axis=-1, keepdims=True) + LN_EPS)
        g = g_ref[pl.ds(q0, t), :].astype(F32)
        o_ref[pl.ds(q0, t), :] = (o * gain * out_scale * _silu(g)).astype(o_ref.dtype)
        return 0

    lax.fori_loop(0, seq // t, qblock, 0)


def _diff_attention(proj, gain, lamv, slopes, batch, seq, *, lambda_init, tile=256):
    blk = lambda base: pl.BlockSpec((seq, HEAD_DIM), lambda b, h: (b, base + h))
    return pl.pallas_call(
        functools.partial(_diff_kernel, tile=tile, out_scale=1.0 - lambda_init, lambda_init=lambda_init),
        grid=(batch, N_DIFF),
        in_specs=[pl.BlockSpec(memory_space=pltpu.SMEM),
                  pl.BlockSpec((4, HEAD_DIM // 2), lambda b, h: (0, 0)),
                  blk(DIFF_Q), blk(DIFF_K), blk(DIFF_V), blk(DIFF_G),
                  pl.BlockSpec((1, HEAD_DIM), lambda b, h: (0, h))],
        out_specs=pl.BlockSpec((seq, HEAD_DIM), lambda b, h: (b, h)),
        out_shape=jax.ShapeDtypeStruct((batch * seq, DIFF_W), BF16),
        compiler_params=pltpu.CompilerParams(
            dimension_semantics=("parallel", "parallel"), vmem_limit_bytes=V7X_VMEM_LIMIT),
        name="diff_attention",
    )(slopes, lamv, proj, proj, proj, proj, gain)


def _outproj_kernel(m_ref, w_ref, x_ref, o_ref, *, alpha):
    o_ref[...] = alpha * x_ref[...] + _dot(m_ref[...], w_ref[...])


def _outproj(mixed, wb, x, *, alpha, tm=1024, tn=1024):
    m, k = mixed.shape
    n = wb.shape[1]
    return pl.pallas_call(
        functools.partial(_outproj_kernel, alpha=alpha),
        grid=(m // tm, n // tn),
        in_specs=[pl.BlockSpec((tm, k), lambda i, j: (i, 0)),
                  pl.BlockSpec((k, tn), lambda i, j: (0, j)),
                  pl.BlockSpec((tm, tn), lambda i, j: (i, j))],
        out_specs=pl.BlockSpec((tm, tn), lambda i, j: (i, j)),
        out_shape=jax.ShapeDtypeStruct((m, n), F32),
        compiler_params=pltpu.CompilerParams(
            dimension_semantics=("parallel", "parallel"), vmem_limit_bytes=V7X_VMEM_LIMIT),
        name="out_proj_residual",
    )(mixed, wb, x)


def _ln_kernel(z_ref, g_ref, b_ref, o_ref, ob_ref):
    z = z_ref[...]
    mu = jnp.mean(z, axis=-1, keepdims=True)
    zc = z - mu
    var = jnp.mean(zc * zc, axis=-1, keepdims=True)
    y = zc * lax.rsqrt(var + LN_EPS) * g_ref[...] + b_ref[...]
    o_ref[...] = y
    ob_ref[...] = y.astype(ob_ref.dtype)


def _layer_norm(z, g, b, *, tr=256):
    m, d = z.shape
    return pl.pallas_call(
        _ln_kernel,
        grid=(m // tr,),
        in_specs=[pl.BlockSpec((tr, d), lambda i: (i, 0)),
                  pl.BlockSpec((1, d), lambda i: (0, 0)),
                  pl.BlockSpec((1, d), lambda i: (0, 0))],
        out_specs=[pl.BlockSpec((tr, d), lambda i: (i, 0)),
                   pl.BlockSpec((tr, d), lambda i: (i, 0))],
        out_shape=[jax.ShapeDtypeStruct((m, d), F32), jax.ShapeDtypeStruct((m, d), BF16)],
        compiler_params=pltpu.CompilerParams(
            dimension_semantics=("parallel",), vmem_limit_bytes=V7X_VMEM_LIMIT),
        name="layer_norm",
    )(z, g, b)


def kernel(x, w_in, w_out, ln_g, ln_b, ret_gn_g, diff_ln_g, lam_q1, lam_k1, lam_q2, lam_k2):
    batch, seq, d_model = x.shape
    depth = w_in.shape[0]
    alpha = (2.0 * depth) ** 0.25
    log_gamma = jnp.asarray(_retention_log_decay(N_RET))
    slopes = jnp.asarray(_alibi_slopes(N_DIFF))

    xf = x.reshape(batch * seq, d_model)
    xb = xf.astype(BF16)
    for layer in range(depth):
        lambda_init = 0.8 - 0.6 * math.exp(-0.3 * layer)
        proj = _proj(xb, w_in[layer].astype(BF16))
        ro = _retention(proj, ret_gn_g[layer][None, :], log_gamma, batch, seq)
        so = _stick_breaking(proj, batch, seq)
        lamv = jnp.stack([lam_q1[layer], lam_k1[layer], lam_q2[layer], lam_k2[layer]]).astype(F32)
        do = _diff_attention(proj, diff_ln_g[layer][None, :], lamv, slopes, batch, seq,
                             lambda_init=lambda_init)
        mixed = jnp.concatenate([ro, so, do], axis=-1)
        z = _outproj(mixed, w_out[layer].astype(BF16), xf, alpha=alpha)
        xf, xb = _layer_norm(z, ln_g[layer][None, :], ln_b[layer][None, :])
    return xf.reshape(batch, seq, d_model)
```

```python
import functools
import math

import jax
import jax.numpy as jnp
import numpy as np
from jax import lax
from jax.experimental import pallas as pl
from jax.experimental.pallas import tpu as pltpu

HEAD_DIM = 128
N_RET = 12
N_SB = 10
N_DIFF = 10
RET_W = N_RET * HEAD_DIM
SB_W = N_SB * HEAD_DIM
DIFF_W = N_DIFF * HEAD_DIM
PROJ_W = 4 * (RET_W + SB_W + DIFF_W)
CHUNK = 128
LN_EPS = 1e-5
LOG2E = math.log2(math.e)

RET_Q, RET_K, RET_V, RET_G = 0, N_RET, 2 * N_RET, 3 * N_RET
SB_BASE = 4 * N_RET
SB_Q, SB_K, SB_V, SB_G = SB_BASE, SB_BASE + N_SB, SB_BASE + 2 * N_SB, SB_BASE + 3 * N_SB
DIFF_BASE = SB_BASE + 4 * N_SB
DIFF_Q, DIFF_K, DIFF_V, DIFF_G = (DIFF_BASE, DIFF_BASE + N_DIFF, DIFF_BASE + 2 * N_DIFF,
                                  DIFF_BASE + 3 * N_DIFF)

RET_HEADS_PER_STEP = 4
ATT_TQ = 1024
ATT_TK = 256
NEG = -1e30

V7X_VMEM_LIMIT = 56 * 1024 * 1024

F32 = jnp.float32
BF16 = jnp.bfloat16


def _alibi_slopes(n):
    def pow2(m):
        start = 2.0 ** (-8.0 / m)
        return [start ** (i + 1) for i in range(m)]
    if math.log2(n).is_integer():
        s = pow2(n)
    else:
        c = 2 ** math.floor(math.log2(n))
        s = pow2(c) + pow2(2 * c)[0::2][: n - c]
    return np.asarray(s, dtype=np.float32)


def _retention_log_decay(n):
    g = 1.0 - 2.0 ** (-5.0 - np.arange(n))
    return np.log(g).astype(np.float32)


def _proj_col_scale():
    s = np.ones((1, PROJ_W), np.float32)
    col = lambda blk: slice(blk * HEAD_DIM, (blk + 1) * HEAD_DIM)
    for h in range(N_RET):
        s[0, col(RET_K + h)] = HEAD_DIM ** -0.5
    for h in range(N_SB):
        s[0, col(SB_Q + h)] = LOG2E * HEAD_DIM ** -0.5
    for h in range(N_DIFF):
        s[0, col(DIFF_Q + h)] = LOG2E * (HEAD_DIM // 2) ** -0.5
    return s


def _silu(g):
    return g * (1.0 / (1.0 + jnp.exp(-g)))


def _dot_nt(a, b):
    return lax.dot_general(a, b, (((1,), (1,)), ((), ())), preferred_element_type=F32)


def _dot(a, b):
    return jnp.dot(a, b, preferred_element_type=F32)


def _neg_abs(x):
    bits = lax.bitcast_convert_type(x, jnp.uint32) | jnp.uint32(0x80000000)
    return lax.bitcast_convert_type(bits, F32)


def _lanes2(x):
    return jnp.concatenate([x, x], axis=1)


def _params(*sem):
    return pltpu.CompilerParams(dimension_semantics=sem, vmem_limit_bytes=V7X_VMEM_LIMIT)


def _proj_kernel(x_ref, w_ref, s_ref, o_ref):
    o_ref[...] = (_dot(x_ref[...], w_ref[...]) * s_ref[...]).astype(o_ref.dtype)


def _proj(xb, w_all, layer, col_scale, *, tm=1024, tn=1024):
    m, k = xb.shape
    n = w_all.shape[2]
    return pl.pallas_call(
        _proj_kernel,
        grid=(m // tm, n // tn),
        in_specs=[pl.BlockSpec((tm, k), lambda i, j: (i, 0)),
                  pl.BlockSpec((None, k, tn), lambda i, j: (layer, 0, j)),
                  pl.BlockSpec((1, tn), lambda i, j: (0, j))],
        out_specs=pl.BlockSpec((tm, tn), lambda i, j: (i, j)),
        out_shape=jax.ShapeDtypeStruct((m, n), BF16),
        compiler_params=_params("parallel", "parallel"),
        name="proj_matmul",
    )(xb, w_all, col_scale)


def _ret_kernel(lg_ref, q_ref, k_ref, v_ref, g_ref, gn_ref, o_ref,
                state_ref, inner_ref, qd_ref, kd_ref, cd_ref):
    hg = pl.program_id(1)
    nh = RET_HEADS_PER_STEP
    seq = q_ref.shape[0]
    c = CHUNK
    row = lax.broadcasted_iota(jnp.int32, (c, c), 0)
    col = lax.broadcasted_iota(jnp.int32, (c, c), 1)
    rel = (row - col).astype(F32)
    causal = row >= col
    rowf = row.astype(F32)
    for i in range(nh):
        lg = lg_ref[hg * nh + i]
        inner_ref[i] = jnp.where(causal, jnp.exp(lg * jnp.where(causal, rel, 0.0)), 0.0)
        qd_ref[i] = jnp.exp(lg * (rowf + 1.0))
        kd_ref[i] = jnp.exp(lg * (c - 1.0 - rowf))
        cd_ref[i] = jnp.exp(lg * (c + 0.0 * rowf))
    state_ref[...] = jnp.zeros_like(state_ref)

    def body(ci, carry):
        r0 = pl.multiple_of(ci * c, c)
        for i in range(nh):
            lanes = slice(i * HEAD_DIM, (i + 1) * HEAD_DIM)
            q = q_ref[pl.ds(r0, c), lanes]
            k = k_ref[pl.ds(r0, c), lanes]
            v = v_ref[pl.ds(r0, c), lanes]
            g = g_ref[pl.ds(r0, c), lanes].astype(F32)
            state = state_ref[i]
            scores = _dot_nt(q, k) * inner_ref[i]
            o = _dot(scores.astype(BF16), v)
            o = o + _dot(q, state.astype(BF16)) * qd_ref[i]
            kd = (k.astype(F32) * kd_ref[i]).astype(BF16)
            upd = lax.dot_general(kd, v, (((0,), (0,)), ((), ())), preferred_element_type=F32)
            state_ref[i] = state * cd_ref[i] + upd
            mu = jnp.mean(o, axis=-1, keepdims=True)
            oc = o - mu
            var = jnp.mean(oc * oc, axis=-1, keepdims=True)
            y = oc * lax.rsqrt(var + LN_EPS) * gn_ref[:, lanes] * _silu(g)
            o_ref[pl.ds(r0, c), lanes] = y.astype(o_ref.dtype)
        return carry

    lax.fori_loop(0, seq // c, body, 0)


def _retention(proj, gn_g, log_gamma, batch, seq):
    nh = RET_HEADS_PER_STEP
    w = nh * HEAD_DIM
    blk = lambda base: pl.BlockSpec((seq, w), lambda b, h: (b, base // nh + h))
    tab = pltpu.VMEM((nh, CHUNK, CHUNK), F32)
    return pl.pallas_call(
        _ret_kernel,
        grid=(batch, N_RET // nh),
        in_specs=[pl.BlockSpec(memory_space=pltpu.SMEM),
                  blk(RET_Q), blk(RET_K), blk(RET_V), blk(RET_G),
                  pl.BlockSpec((1, w), lambda b, h: (0, h))],
        out_specs=pl.BlockSpec((seq, w), lambda b, h: (b, h)),
        out_shape=jax.ShapeDtypeStruct((batch * seq, RET_W), BF16),
        scratch_shapes=[pltpu.VMEM((nh, HEAD_DIM, HEAD_DIM), F32), tab, tab, tab, tab],
        compiler_params=_params("parallel", "parallel"),
        name="retention",
    )(log_gamma, proj, proj, proj, proj, gn_g)


def _sb_kernel(q_ref, k_ref, v_ref, g_ref, o_ref, acc_ref, car_ref):
    seq = q_ref.shape[0]
    tq, tk = ATT_TQ, ATT_TK
    sub = tq // tk
    row = lax.broadcasted_iota(jnp.int32, (tk, tk), 0)
    col = lax.broadcasted_iota(jnp.int32, (tk, tk), 1)
    strict = col < row
    upper = jnp.where(row >= col, 1.0, 0.0).astype(BF16)
    upper2 = jnp.concatenate([upper, upper], axis=0)

    def block(q0, r0, rows, k0, masked):
        q = q_ref[pl.ds(q0 + r0, rows), :]
        k = k_ref[pl.ds(k0, tk), :]
        v = v_ref[pl.ds(k0, tk), :]
        z = _dot_nt(q, k)
        e = jnp.exp2(_neg_abs(z))
        nl = jnp.maximum(z, 0.0) + jnp.log2(1.0 + e)
        if masked:
            nl = jnp.where(strict, nl, 0.0)
        hi = nl.astype(BF16)
        lo = (nl - hi.astype(F32)).astype(BF16)
        car = car_ref[pl.ds(r0, rows), :]
        tail = _dot(jnp.concatenate([hi, lo], axis=1), upper2) + _lanes2(car)
        a = jnp.exp2(z - tail)
        if masked:
            a = jnp.where(strict, a, 0.0)
        acc_ref[pl.ds(r0, rows), :] += _dot(a.astype(BF16), v)
        car_ref[pl.ds(r0, rows), :] = car + jnp.broadcast_to(
            jnp.sum(nl, axis=-1, keepdims=True), (rows, HEAD_DIM))

    def qtile(qi, _):
        q0 = pl.multiple_of(qi * tq, tq)
        acc_ref[...] = jnp.zeros_like(acc_ref)
        car_ref[...] = jnp.zeros_like(car_ref)
        for jd in reversed(range(sub)):
            k0 = q0 + jd * tk
            block(q0, jd * tk, tk, k0, True)
            if jd + 1 < sub:
                block(q0, (jd + 1) * tk, tq - (jd + 1) * tk, k0, False)

        def kblock(jj, c):
            k0 = pl.multiple_of((qi * sub - 1 - jj) * tk, tk)
            block(q0, 0, tq, k0, False)
            return c

        lax.fori_loop(0, qi * sub, kblock, 0)
        g = g_ref[pl.ds(q0, tq), :].astype(F32)
        o_ref[pl.ds(q0, tq), :] = (acc_ref[...] * _silu(g)).astype(o_ref.dtype)
        return 0

    lax.fori_loop(0, seq // tq, qtile, 0)


def _stick_breaking(proj, batch, seq):
    blk = lambda base: pl.BlockSpec((seq, HEAD_DIM), lambda b, h: (b, base + h))
    return pl.pallas_call(
        _sb_kernel,
        grid=(batch, N_SB),
        in_specs=[blk(SB_Q), blk(SB_K), blk(SB_V), blk(SB_G)],
        out_specs=pl.BlockSpec((seq, HEAD_DIM), lambda b, h: (b, h)),
        out_shape=jax.ShapeDtypeStruct((batch * seq, SB_W), BF16),
        scratch_shapes=[pltpu.VMEM((ATT_TQ, HEAD_DIM), F32), pltpu.VMEM((ATT_TQ, HEAD_DIM), F32)],
        compiler_params=_params("parallel", "parallel"),
        name="stick_breaking",
    )(proj, proj, proj, proj)


def _diff_kernel(slope_ref, lamv_ref, q_ref, k_ref, v_ref, g_ref, gain_ref, o_ref,
                 q1_ref, q2_ref, accl1_ref, accl2_ref, m1_ref, m2_ref, *, out_scale, lambda_init):
    h = pl.program_id(1)
    sl2 = slope_ref[h] * LOG2E
    seq = q_ref.shape[0]
    tq, tk = ATT_TQ, ATT_TK
    sub = tq // tk
    half = HEAD_DIM // 2
    row = lax.broadcasted_iota(jnp.int32, (tk, tk), 0)
    col = lax.broadcasted_iota(jnp.int32, (tk, tk), 1)
    causal = col <= row
    col_bias = lax.broadcasted_iota(jnp.int32, (1, tk), 1).astype(F32) * sl2
    lane = lax.broadcasted_iota(jnp.int32, (tq, HEAD_DIM), 1)
    first_half = lane < half
    ones = jnp.ones((tk, HEAD_DIM), BF16)

    lv = lamv_ref[...]
    lam = (jnp.exp(jnp.sum(lv[0:1] * lv[1:2], axis=-1, keepdims=True))
           - jnp.exp(jnp.sum(lv[2:3] * lv[3:4], axis=-1, keepdims=True)) + lambda_init)

    maps = ((q1_ref, m1_ref, accl1_ref), (q2_ref, m2_ref, accl2_ref))

    def block(q0, r0, rows, k0, masked):
        k = k_ref[pl.ds(k0, tk), :]
        v1 = jnp.concatenate([v_ref[pl.ds(k0, tk), :], ones], axis=1)
        cj = sl2 * (k0 - q0).astype(F32)
        rs = pl.ds(r0, rows)
        for qm_ref, m_ref, accl_ref in maps:
            y = _dot_nt(qm_ref[rs, :], k) + col_bias
            if masked:
                y = jnp.where(causal, y, NEG)
            m_old = m_ref[rs, :]
            m_blk = jnp.broadcast_to(jnp.max(y, axis=-1, keepdims=True), (rows, HEAD_DIM))
            m_new = jnp.maximum(m_old, m_blk + cj)
            p = jnp.exp2(y - _lanes2(m_new - cj))
            alpha = jnp.exp2(m_old - m_new)
            accl_ref[rs, :] = _lanes2(alpha) * accl_ref[rs, :] + _dot(p.astype(BF16), v1)
            m_ref[rs, :] = m_new

    def qtile(qi, _):
        q0 = pl.multiple_of(qi * tq, tq)
        q = q_ref[pl.ds(q0, tq), :]
        zero = jnp.zeros_like(q)
        q1_ref[...] = jnp.where(first_half, q, zero)
        q2_ref[...] = jnp.where(first_half, zero, q)
        for _, m_ref, accl_ref in maps:
            m_ref[...] = jnp.full_like(m_ref, NEG)
            accl_ref[...] = jnp.zeros_like(accl_ref)
        for jd in range(sub):
            k0 = q0 + jd * tk
            block(q0, jd * tk, tk, k0, True)
            if jd + 1 < sub:
                block(q0, (jd + 1) * tk, tq - (jd + 1) * tk, k0, False)

        def kblock(j, c):
            block(q0, 0, tq, pl.multiple_of(j * tk, tk), False)
            return c

        lax.fori_loop(0, qi * sub, kblock, 0)
        o1 = accl1_ref[:, :HEAD_DIM] * (1.0 / accl1_ref[:, HEAD_DIM:])
        o2 = accl2_ref[:, :HEAD_DIM] * (1.0 / accl2_ref[:, HEAD_DIM:])
        o = o1 - lam * o2
        o = o * lax.rsqrt(jnp.mean(o * o, axis=-1, keepdims=True) + LN_EPS)
        g = g_ref[pl.ds(q0, tq), :].astype(F32)
        o_ref[pl.ds(q0, tq), :] = (o * (gain_ref[...] * out_scale) * _silu(g)).astype(o_ref.dtype)
        return 0

    lax.fori_loop(0, seq // tq, qtile, 0)


def _diff_attention(proj, gain, lamv, slopes, batch, seq, *, lambda_init):
    blk = lambda base: pl.BlockSpec((seq, HEAD_DIM), lambda b, h: (b, base + h))
    qm = pltpu.VMEM((ATT_TQ, HEAD_DIM), BF16)
    accl = pltpu.VMEM((ATT_TQ, 2 * HEAD_DIM), F32)
    stat = pltpu.VMEM((ATT_TQ, HEAD_DIM), F32)
    return pl.pallas_call(
        functools.partial(_diff_kernel, out_scale=1.0 - lambda_init, lambda_init=lambda_init),
        grid=(batch, N_DIFF),
        in_specs=[pl.BlockSpec(memory_space=pltpu.SMEM),
                  pl.BlockSpec((4, HEAD_DIM // 2), lambda b, h: (0, 0)),
                  blk(DIFF_Q), blk(DIFF_K), blk(DIFF_V), blk(DIFF_G),
                  pl.BlockSpec((1, HEAD_DIM), lambda b, h: (0, h))],
        out_specs=pl.BlockSpec((seq, HEAD_DIM), lambda b, h: (b, h)),
        out_shape=jax.ShapeDtypeStruct((batch * seq, DIFF_W), BF16),
        scratch_shapes=[qm, qm, accl, accl, stat, stat],
        compiler_params=_params("parallel", "parallel"),
        name="diff_attention",
    )(slopes, lamv, proj, proj, proj, proj, gain)


def _outproj_kernel(ro_ref, so_ref, do_ref, w_ref, x_ref, o_ref, *, alpha):
    y = _dot(ro_ref[...], w_ref[0:RET_W, :])
    y = y + _dot(so_ref[...], w_ref[RET_W:RET_W + SB_W, :])
    y = y + _dot(do_ref[...], w_ref[RET_W + SB_W:, :])
    o_ref[...] = alpha * x_ref[...] + y


def _outproj(ro, so, do, w_all, layer, x, *, alpha, tm=1024, tn=1024):
    m = ro.shape[0]
    k, n = w_all.shape[1], w_all.shape[2]
    row_blk = lambda w: pl.BlockSpec((tm, w), lambda i, j: (i, 0))
    return pl.pallas_call(
        functools.partial(_outproj_kernel, alpha=alpha),
        grid=(m // tm, n // tn),
        in_specs=[row_blk(RET_W), row_blk(SB_W), row_blk(DIFF_W),
                  pl.BlockSpec((None, k, tn), lambda i, j: (layer, 0, j)),
                  pl.BlockSpec((tm, tn), lambda i, j: (i, j))],
        out_specs=pl.BlockSpec((tm, tn), lambda i, j: (i, j)),
        out_shape=jax.ShapeDtypeStruct((m, n), F32),
        compiler_params=_params("parallel", "parallel"),
        name="out_proj_residual",
    )(ro, so, do, w_all, x)


def _ln_kernel(z_ref, g_ref, b_ref, o_ref, ob_ref):
    z = z_ref[...]
    mu = jnp.mean(z, axis=-1, keepdims=True)
    zc = z - mu
    var = jnp.mean(zc * zc, axis=-1, keepdims=True)
    y = zc * lax.rsqrt(var + LN_EPS) * g_ref[...] + b_ref[...]
    o_ref[...] = y
    ob_ref[...] = y.astype(ob_ref.dtype)


def _layer_norm(z, g, b, *, tr=256):
    m, d = z.shape
    return pl.pallas_call(
        _ln_kernel,
        grid=(m // tr,),
        in_specs=[pl.BlockSpec((tr, d), lambda i: (i, 0)),
                  pl.BlockSpec((1, d), lambda i: (0, 0)),
                  pl.BlockSpec((1, d), lambda i: (0, 0))],
        out_specs=[pl.BlockSpec((tr, d), lambda i: (i, 0)),
                   pl.BlockSpec((tr, d), lambda i: (i, 0))],
        out_shape=[jax.ShapeDtypeStruct((m, d), F32), jax.ShapeDtypeStruct((m, d), BF16)],
        compiler_params=_params("parallel"),
        name="layer_norm",
    )(z, g, b)


def kernel(x, w_in, w_out, ln_g, ln_b, ret_gn_g, diff_ln_g, lam_q1, lam_k1, lam_q2, lam_k2):
    batch, seq, d_model = x.shape
    depth = w_in.shape[0]
    alpha = (2.0 * depth) ** 0.25
    log_gamma = jnp.asarray(_retention_log_decay(N_RET))
    slopes = jnp.asarray(_alibi_slopes(N_DIFF))
    col_scale = jnp.asarray(_proj_col_scale())
    w_in_b = w_in.astype(BF16)
    w_out_b = w_out.astype(BF16)

    xf = x.reshape(batch * seq, d_model)
    xb = xf.astype(BF16)
    for layer in range(depth):
        lambda_init = 0.8 - 0.6 * math.exp(-0.3 * layer)
        proj = _proj(xb, w_in_b, layer, col_scale)
        ro = _retention(proj, ret_gn_g[layer][None, :], log_gamma, batch, seq)
        so = _stick_breaking(proj, batch, seq)
        lamv = jnp.stack([lam_q1[layer], lam_k1[layer], lam_q2[layer], lam_k2[layer]]).astype(F32)
        do = _diff_attention(proj, diff_ln_g[layer][None, :], lamv, slopes, batch, seq,
                             lambda_init=lambda_init)
        z = _outproj(ro, so, do, w_out_b, layer, xf, alpha=alpha)
        xf, xb = _layer_norm(z, ln_g[layer][None, :], ln_b[layer][None, :])
    return xf.reshape(batch, seq, d_model)
```

```python
import functools
import math

import jax
import jax.numpy as jnp
import numpy as np
from jax import lax
from jax.experimental import pallas as pl
from jax.experimental.pallas import tpu as pltpu

HEAD_DIM = 128
N_RET = 12
N_SB = 10
N_DIFF = 10
RET_W = N_RET * HEAD_DIM
SB_W = N_SB * HEAD_DIM
DIFF_W = N_DIFF * HEAD_DIM
PROJ_W = 4 * (RET_W + SB_W + DIFF_W)
CHUNK = 128
LN_EPS = 1e-5
LOG2E = math.log2(math.e)

RET_Q, RET_K, RET_V, RET_G = 0, N_RET, 2 * N_RET, 3 * N_RET
SB_BASE = 4 * N_RET
SB_Q, SB_K, SB_V, SB_G = SB_BASE, SB_BASE + N_SB, SB_BASE + 2 * N_SB, SB_BASE + 3 * N_SB
DIFF_BASE = SB_BASE + 4 * N_SB
DIFF_Q, DIFF_K, DIFF_V, DIFF_G = (DIFF_BASE, DIFF_BASE + N_DIFF, DIFF_BASE + 2 * N_DIFF,
                                  DIFF_BASE + 3 * N_DIFF)

RET_HEADS_PER_STEP = 4
ATT_TQ = 1024
ATT_TK = 256
NEG = -1e30

V7X_VMEM_LIMIT = 56 * 1024 * 1024

F32 = jnp.float32
BF16 = jnp.bfloat16


def _alibi_slopes(n):
    def pow2(m):
        start = 2.0 ** (-8.0 / m)
        return [start ** (i + 1) for i in range(m)]
    if math.log2(n).is_integer():
        s = pow2(n)
    else:
        c = 2 ** math.floor(math.log2(n))
        s = pow2(c) + pow2(2 * c)[0::2][: n - c]
    return np.asarray(s, dtype=np.float32)


def _retention_log_decay(n):
    g = 1.0 - 2.0 ** (-5.0 - np.arange(n))
    return np.log(g).astype(np.float32)


def _proj_col_scale():
    s = np.ones((1, PROJ_W), np.float32)
    col = lambda blk: slice(blk * HEAD_DIM, (blk + 1) * HEAD_DIM)
    for h in range(N_RET):
        s[0, col(RET_K + h)] = HEAD_DIM ** -0.5
    for h in range(N_SB):
        s[0, col(SB_Q + h)] = LOG2E * HEAD_DIM ** -0.5
    for h in range(N_DIFF):
        s[0, col(DIFF_Q + h)] = LOG2E * (HEAD_DIM // 2) ** -0.5
    return s


def _silu(g):
    return g * (1.0 / (1.0 + jnp.exp(-g)))


def _dot_nt(a, b):
    return lax.dot_general(a, b, (((1,), (1,)), ((), ())), preferred_element_type=F32)


def _dot(a, b):
    return jnp.dot(a, b, preferred_element_type=F32)


def _neg_abs(x):
    bits = lax.bitcast_convert_type(x, jnp.uint32) | jnp.uint32(0x80000000)
    return lax.bitcast_convert_type(bits, F32)


def _lanes2(x):
    return jnp.concatenate([x, x], axis=1)


def _params(*sem, flags=None):
    return pltpu.CompilerParams(dimension_semantics=sem, vmem_limit_bytes=V7X_VMEM_LIMIT, flags=flags)


def _proj_kernel(x_ref, w_ref, s_ref, o_ref):
    w = w_ref[...].astype(BF16)
    o_ref[...] = (_dot(x_ref[...], w) * s_ref[...]).astype(o_ref.dtype)


def _proj(xb, w_all, layer, col_scale, *, tm=1024, tn=512):
    m, k = xb.shape
    n = w_all.shape[2]
    return pl.pallas_call(
        _proj_kernel,
        grid=(m // tm, n // tn),
        in_specs=[pl.BlockSpec((tm, k), lambda i, j: (i, 0)),
                  pl.BlockSpec((None, k, tn), lambda i, j: (layer, 0, j)),
                  pl.BlockSpec((1, tn), lambda i, j: (0, j))],
        out_specs=pl.BlockSpec((tm, tn), lambda i, j: (i, j)),
        out_shape=jax.ShapeDtypeStruct((m, n), BF16),
        compiler_params=_params("parallel", "parallel"),
        name="proj_matmul",
    )(xb, w_all, col_scale)


def _ret_kernel(lg_ref, q_ref, k_ref, v_ref, g_ref, gn_ref, o_ref,
                state_ref, inner_ref, qd_ref, kd_ref, cd_ref):
    hg = pl.program_id(1)
    nh = RET_HEADS_PER_STEP
    seq = q_ref.shape[0]
    c = CHUNK
    row = lax.broadcasted_iota(jnp.int32, (c, c), 0)
    col = lax.broadcasted_iota(jnp.int32, (c, c), 1)
    rel = (row - col).astype(F32)
    causal = row >= col
    rowf = row.astype(F32)
    for i in range(nh):
        lg = lg_ref[hg * nh + i]
        inner_ref[i] = jnp.where(causal, jnp.exp(lg * jnp.where(causal, rel, 0.0)), 0.0)
        qd_ref[i] = jnp.exp(lg * (rowf + 1.0))
        kd_ref[i] = jnp.exp(lg * (c - 1.0 - rowf))
        cd_ref[i] = jnp.exp(lg * (c + 0.0 * rowf))
    state_ref[...] = jnp.zeros_like(state_ref)

    def body(ci, carry):
        r0 = pl.multiple_of(ci * c, c)
        for i in range(nh):
            lanes = slice(i * HEAD_DIM, (i + 1) * HEAD_DIM)
            q = q_ref[pl.ds(r0, c), lanes]
            k = k_ref[pl.ds(r0, c), lanes]
            v = v_ref[pl.ds(r0, c), lanes]
            g = g_ref[pl.ds(r0, c), lanes].astype(F32)
            state = state_ref[i]
            scores = _dot_nt(q, k) * inner_ref[i]
            o = _dot(scores.astype(BF16), v)
            o = o + _dot(q, state.astype(BF16)) * qd_ref[i]
            kd = (k.astype(F32) * kd_ref[i]).astype(BF16)
            upd = lax.dot_general(kd, v, (((0,), (0,)), ((), ())), preferred_element_type=F32)
            state_ref[i] = state * cd_ref[i] + upd
            mu = jnp.mean(o, axis=-1, keepdims=True)
            oc = o - mu
            var = jnp.mean(oc * oc, axis=-1, keepdims=True)
            y = oc * lax.rsqrt(var + LN_EPS) * gn_ref[:, lanes] * _silu(g)
            o_ref[pl.ds(r0, c), lanes] = y.astype(o_ref.dtype)
        return carry

    lax.fori_loop(0, seq // c, body, 0, unroll=2)


def _retention(proj, gn_g, log_gamma, batch, seq):
    nh = RET_HEADS_PER_STEP
    w = nh * HEAD_DIM
    blk = lambda base: pl.BlockSpec((seq, w), lambda b, h: (b, base // nh + h))
    tab = pltpu.VMEM((nh, CHUNK, CHUNK), F32)
    return pl.pallas_call(
        _ret_kernel,
        grid=(batch, N_RET // nh),
        in_specs=[pl.BlockSpec(memory_space=pltpu.SMEM),
                  blk(RET_Q), blk(RET_K), blk(RET_V), blk(RET_G),
                  pl.BlockSpec((1, w), lambda b, h: (0, h))],
        out_specs=pl.BlockSpec((seq, w), lambda b, h: (b, h)),
        out_shape=jax.ShapeDtypeStruct((batch * seq, RET_W), BF16),
        scratch_shapes=[pltpu.VMEM((nh, HEAD_DIM, HEAD_DIM), F32), tab, tab, tab, tab],
        compiler_params=_params("parallel", "parallel"),
        name="retention",
    )(log_gamma, proj, proj, proj, proj, gn_g)


def _sb_kernel(q_ref, k_ref, v_ref, g_ref, o_ref, acc_ref, car_ref):
    seq = q_ref.shape[0]
    tq, tk = ATT_TQ, ATT_TK
    sub = tq // tk
    row = lax.broadcasted_iota(jnp.int32, (tk, tk), 0)
    col = lax.broadcasted_iota(jnp.int32, (tk, tk), 1)
    upper = jnp.where(row >= col, 1.0, 0.0).astype(BF16)
    upper2 = jnp.concatenate([upper, upper], axis=0)
    row_in_blk = lax.broadcasted_iota(jnp.int32, (tq, tk), 0) & (tk - 1)
    strict = lax.broadcasted_iota(jnp.int32, (tq, tk), 1) < row_in_blk

    def stage1(z, mask):
        rows = z.shape[0]
        e = jnp.exp2(_neg_abs(z))
        nl = jnp.maximum(z, 0.0) + jnp.log2(1.0 + e)
        if mask is not None:
            nl = jnp.where(mask, nl, 0.0)
        hi = nl.astype(BF16)
        lo = (nl - hi.astype(F32)).astype(BF16)
        rsum = jnp.broadcast_to(jnp.sum(nl, axis=-1, keepdims=True), (rows, HEAD_DIM))
        return jnp.concatenate([hi, lo], axis=1), rsum

    def stage2(z, hl, rsum, r0, mask, v_blocks):
        rows = z.shape[0]
        rs = pl.ds(r0, rows)
        car = car_ref[rs, :]
        a = jnp.exp2(z - (_dot(hl, upper2) + _lanes2(car)))
        if mask is not None:
            a = jnp.where(mask, a, 0.0)
        a = a.astype(BF16)
        if len(v_blocks) == 1:
            upd = _dot(a, v_blocks[0])
        else:
            upd = jnp.concatenate(
                [_dot(a[i * tk:(i + 1) * tk], v) for i, v in enumerate(v_blocks)], axis=0)
        acc_ref[rs, :] += upd
        car_ref[rs, :] = car + rsum

    def qtile(qi, _):
        q0 = pl.multiple_of(qi * tq, tq)
        acc_ref[...] = jnp.zeros_like(acc_ref)
        car_ref[...] = jnp.zeros_like(car_ref)

        def diag1(d):
            n = sub - d
            q = q_ref[pl.ds(q0 + d * tk, n * tk), :]
            z = jnp.concatenate(
                [_dot_nt(q[i * tk:(i + 1) * tk], k_ref[pl.ds(q0 + i * tk, tk), :]) for i in range(n)],
                axis=0)
            mask = strict if d == 0 else None
            return (z,) + stage1(z, mask)

        def diag2(d, staged):
            n = sub - d
            z, hl, rsum = staged
            mask = strict if d == 0 else None
            stage2(z, hl, rsum, d * tk, mask, [v_ref[pl.ds(q0 + i * tk, tk), :] for i in range(n)])

        staged = diag1(0)
        for d in range(sub):
            nxt = diag1(d + 1) if d + 1 < sub else None
            diag2(d, staged)
            staged = nxt

        def ktile(i, c):
            for j in range(sub):
                k0 = pl.multiple_of(((qi - i) * sub - 1 - j) * tk, tk)
                z = _dot_nt(q_ref[pl.ds(q0, tq), :], k_ref[pl.ds(k0, tk), :])
                hl, rsum = stage1(z, None)
                stage2(z, hl, rsum, 0, None, [v_ref[pl.ds(k0, tk), :]])
            return c

        lax.fori_loop(0, qi, ktile, 0)

        g = g_ref[pl.ds(q0, tq), :].astype(F32)
        o_ref[pl.ds(q0, tq), :] = (acc_ref[...] * _silu(g)).astype(o_ref.dtype)
        return 0

    lax.fori_loop(0, seq // tq, qtile, 0)


def _stick_breaking(proj, batch, seq):
    blk = lambda base: pl.BlockSpec((seq, HEAD_DIM), lambda b, h: (b, base + h))
    rows128 = pltpu.VMEM((ATT_TQ, HEAD_DIM), F32)
    return pl.pallas_call(
        _sb_kernel,
        grid=(batch, N_SB),
        in_specs=[blk(SB_Q), blk(SB_K), blk(SB_V), blk(SB_G)],
        out_specs=pl.BlockSpec((seq, HEAD_DIM), lambda b, h: (b, h)),
        out_shape=jax.ShapeDtypeStruct((batch * seq, SB_W), BF16),
        scratch_shapes=[rows128, rows128],
        compiler_params=_params("parallel", "parallel"),
        name="stick_breaking",
    )(proj, proj, proj, proj)


def _diff_kernel(slope_ref, lamv_ref, q_ref, k_ref, v_ref, g_ref, gain_ref, o_ref,
                 q1_ref, q2_ref, accl1_ref, accl2_ref, m1_ref, m2_ref, *, out_scale, lambda_init):
    h = pl.program_id(1)
    sl2 = slope_ref[h] * LOG2E
    seq = q_ref.shape[0]
    tq, tk = ATT_TQ, ATT_TK
    sub = tq // tk
    half = HEAD_DIM // 2
    causal = (lax.broadcasted_iota(jnp.int32, (tk, tk), 1)
              <= lax.broadcasted_iota(jnp.int32, (tk, tk), 0))
    ones = jnp.ones((tk, HEAD_DIM), BF16)
    frame_shift = sl2 * tk
    lane_q = lax.broadcasted_iota(jnp.int32, (tq, HEAD_DIM), 1)
    lane_k = lax.broadcasted_iota(jnp.int32, (tk, HEAD_DIM), 1)
    s_full = jnp.full((tq, HEAD_DIM), sl2, F32)
    s_hi = s_full.astype(BF16).astype(F32)
    s_mid = (s_full - s_hi).astype(BF16).astype(F32)
    s_lo = (s_full - s_hi - s_mid).astype(BF16).astype(F32)

    def slope_feats(base):
        f = jnp.where(lane_q == base, s_hi,
                      jnp.where(lane_q == base + 1, s_mid, jnp.where(lane_q == base + 2, s_lo, 0.0)))
        return f.astype(BF16)

    key_pos = lax.broadcasted_iota(jnp.int32, (tk, HEAD_DIM), 0).astype(F32)

    def pos_feats(base):
        f = jnp.where(lane_k >= base, jnp.where(lane_k < base + 3, key_pos, 0.0), 0.0)
        return f.astype(BF16)

    def keep(lane, first):
        return jnp.where((lane < half) if first else (lane >= half), 1.0, 0.0).astype(BF16)

    q_feats = (slope_feats(half), slope_feats(0))
    k_feats = (pos_feats(half), pos_feats(0))
    q_keep = (keep(lane_q, True), keep(lane_q, False))
    k_keep = (keep(lane_k, True), keep(lane_k, False))

    lv = lamv_ref[...]
    lam = (jnp.exp(jnp.sum(lv[0:1] * lv[1:2], axis=-1, keepdims=True))
           - jnp.exp(jnp.sum(lv[2:3] * lv[3:4], axis=-1, keepdims=True)) + lambda_init)

    maps = ((q1_ref, m1_ref, accl1_ref), (q2_ref, m2_ref, accl2_ref))

    def block(r0, rows, k_starts, mask):
        rs = pl.ds(r0, rows)
        vs = [jnp.concatenate([v_ref[pl.ds(k0, tk), :], ones], axis=1) for k0 in k_starts]
        for mi, (qm_ref, m_ref, accl_ref) in enumerate(maps):
            parts = []
            for i, k0 in enumerate(k_starts):
                k = k_ref[pl.ds(k0, tk), :] * k_keep[mi] + k_feats[mi]
                rows_i = rows if len(k_starts) == 1 else tk
                parts.append(_dot_nt(qm_ref[pl.ds(r0 + i * tk, rows_i), :], k))
            y = parts[0] if len(parts) == 1 else jnp.concatenate(parts, axis=0)
            if mask is not None:
                y = jnp.where(mask, y, NEG)
            m_blk = jnp.broadcast_to(jnp.max(y, axis=-1, keepdims=True), (rows, HEAD_DIM))
            m_old = m_ref[rs, :] - frame_shift
            m_new = jnp.maximum(m_old, m_blk)
            p = jnp.exp2(y - _lanes2(m_new)).astype(BF16)
            alpha = jnp.exp2(m_old - m_new)
            if len(vs) == 1:
                upd = _dot(p, vs[0])
            else:
                upd = jnp.concatenate(
                    [_dot(p[i * tk:(i + 1) * tk], v) for i, v in enumerate(vs)], axis=0)
            accl_ref[rs, :] = _lanes2(alpha) * accl_ref[rs, :] + upd
            m_ref[rs, :] = m_new

    def qtile(qi, _):
        q0 = pl.multiple_of(qi * tq, tq)
        q = q_ref[pl.ds(q0, tq), :]
        q1_ref[...] = q * q_keep[0] + q_feats[0]
        q2_ref[...] = q * q_keep[1] + q_feats[1]
        for _, m_ref, accl_ref in maps:
            m_ref[...] = jnp.full_like(m_ref, NEG)
            accl_ref[...] = jnp.zeros_like(accl_ref)

        def ktile(i, c):
            for j in range(sub):
                block(0, tq, [pl.multiple_of((sub * i + j) * tk, tk)], None)
            return c

        lax.fori_loop(0, qi, ktile, 0)

        for jd in range(sub):
            k0 = q0 + jd * tk
            block(jd * tk, tk, [k0], causal)
            if jd + 1 < sub:
                block((jd + 1) * tk, tq - (jd + 1) * tk, [k0], None)

        o1 = accl1_ref[:, :HEAD_DIM] * (1.0 / accl1_ref[:, HEAD_DIM:])
        o2 = accl2_ref[:, :HEAD_DIM] * (1.0 / accl2_ref[:, HEAD_DIM:])
        o = o1 - lam * o2
        o = o * lax.rsqrt(jnp.mean(o * o, axis=-1, keepdims=True) + LN_EPS)
        g = g_ref[pl.ds(q0, tq), :].astype(F32)
        o_ref[pl.ds(q0, tq), :] = (o * (gain_ref[...] * out_scale) * _silu(g)).astype(o_ref.dtype)
        return 0

    lax.fori_loop(0, seq // tq, qtile, 0)


def _diff_attention(proj, gain, lamv, slopes, batch, seq, *, lambda_init):
    blk = lambda base: pl.BlockSpec((seq, HEAD_DIM), lambda b, h: (b, base + h))
    qm = pltpu.VMEM((ATT_TQ, HEAD_DIM), BF16)
    accl = pltpu.VMEM((ATT_TQ, 2 * HEAD_DIM), F32)
    stat = pltpu.VMEM((ATT_TQ, HEAD_DIM), F32)
    return pl.pallas_call(
        functools.partial(_diff_kernel, out_scale=1.0 - lambda_init, lambda_init=lambda_init),
        grid=(batch, N_DIFF),
        in_specs=[pl.BlockSpec(memory_space=pltpu.SMEM),
                  pl.BlockSpec((4, HEAD_DIM // 2), lambda b, h: (0, 0)),
                  blk(DIFF_Q), blk(DIFF_K), blk(DIFF_V), blk(DIFF_G),
                  pl.BlockSpec((1, HEAD_DIM), lambda b, h: (0, h))],
        out_specs=pl.BlockSpec((seq, HEAD_DIM), lambda b, h: (b, h)),
        out_shape=jax.ShapeDtypeStruct((batch * seq, DIFF_W), BF16),
        scratch_shapes=[qm, qm, accl, accl, stat, stat],
        compiler_params=_params("parallel", "parallel"),
        name="diff_attention",
    )(slopes, lamv, proj, proj, proj, proj, gain)


def _outproj_kernel(ro_ref, so_ref, do_ref, w_ref, x_ref, o_ref, *, alpha):
    y = _dot(ro_ref[...], w_ref[0:RET_W, :])
    y = y + _dot(so_ref[...], w_ref[RET_W:RET_W + SB_W, :])
    y = y + _dot(do_ref[...], w_ref[RET_W + SB_W:, :])
    o_ref[...] = alpha * x_ref[...] + y


def _outproj(ro, so, do, w_all, layer, x, *, alpha, tm=1024, tn=1024):
    m = ro.shape[0]
    k, n = w_all.shape[1], w_all.shape[2]
    row_blk = lambda w: pl.BlockSpec((tm, w), lambda i, j: (i, 0))
    return pl.pallas_call(
        functools.partial(_outproj_kernel, alpha=alpha),
        grid=(m // tm, n // tn),
        in_specs=[row_blk(RET_W), row_blk(SB_W), row_blk(DIFF_W),
                  pl.BlockSpec((None, k, tn), lambda i, j: (layer, 0, j)),
                  pl.BlockSpec((tm, tn), lambda i, j: (i, j))],
        out_specs=pl.BlockSpec((tm, tn), lambda i, j: (i, j)),
        out_shape=jax.ShapeDtypeStruct((m, n), F32),
        compiler_params=_params("parallel", "parallel"),
        name="out_proj_residual",
    )(ro, so, do, w_all, x)


def _ln_kernel(z_ref, g_ref, b_ref, o_ref, ob_ref):
    z = z_ref[...]
    mu = jnp.mean(z, axis=-1, keepdims=True)
    zc = z - mu
    var = jnp.mean(zc * zc, axis=-1, keepdims=True)
    y = zc * lax.rsqrt(var + LN_EPS) * g_ref[...] + b_ref[...]
    o_ref[...] = y
    ob_ref[...] = y.astype(ob_ref.dtype)


def _layer_norm(z, g, b, *, tr=256):
    m, d = z.shape
    return pl.pallas_call(
        _ln_kernel,
        grid=(m // tr,),
        in_specs=[pl.BlockSpec((tr, d), lambda i: (i, 0)),
                  pl.BlockSpec((1, d), lambda i: (0, 0)),
                  pl.BlockSpec((1, d), lambda i: (0, 0))],
        out_specs=[pl.BlockSpec((tr, d), lambda i: (i, 0)),
                   pl.BlockSpec((tr, d), lambda i: (i, 0))],
        out_shape=[jax.ShapeDtypeStruct((m, d), F32), jax.ShapeDtypeStruct((m, d), BF16)],
        compiler_params=_params("parallel"),
        name="layer_norm",
    )(z, g, b)


def kernel(x, w_in, w_out, ln_g, ln_b, ret_gn_g, diff_ln_g, lam_q1, lam_k1, lam_q2, lam_k2):
    batch, seq, d_model = x.shape
    depth = w_in.shape[0]
    alpha = (2.0 * depth) ** 0.25
    log_gamma = jnp.asarray(_retention_log_decay(N_RET))
    slopes = jnp.asarray(_alibi_slopes(N_DIFF))
    col_scale = jnp.asarray(_proj_col_scale())
    w_out_b = w_out.astype(BF16)

    xf = x.reshape(batch * seq, d_model)
    xb = xf.astype(BF16)
    for layer in range(depth):
        lambda_init = 0.8 - 0.6 * math.exp(-0.3 * layer)
        proj = _proj(xb, w_in, layer, col_scale)
        ro = _retention(proj, ret_gn_g[layer][None, :], log_gamma, batch, seq)
        so = _stick_breaking(proj, batch, seq)
        lamv = jnp.stack([lam_q1[layer], lam_k1[layer], lam_q2[layer], lam_k2[layer]]).astype(F32)
        do = _diff_attention(proj, diff_ln_g[layer][None, :], lamv, slopes, batch, seq,
                             lambda_init=lambda_init)
        z = _outproj(ro, so, do, w_out_b, layer, xf, alpha=alpha)
        xf, xb = _layer_norm(z, ln_g[layer][None, :], ln_b[layer][None, :])
    return xf.reshape(batch, seq, d_model)
```

```python
import functools
import math

import jax
import jax.numpy as jnp
import numpy as np
from jax import lax
from jax.experimental import pallas as pl
from jax.experimental.pallas import tpu as pltpu

HEAD_DIM = 128
N_RET = 12
N_SB = 10
N_DIFF = 10
RET_W = N_RET * HEAD_DIM
SB_W = N_SB * HEAD_DIM
DIFF_W = N_DIFF * HEAD_DIM
PROJ_W = 4 * (RET_W + SB_W + DIFF_W)
CHUNK = 128
LN_EPS = 1e-5
LOG2E = math.log2(math.e)

RET_Q, RET_K, RET_V, RET_G = 0, N_RET, 2 * N_RET, 3 * N_RET
SB_BASE = 4 * N_RET
SB_Q, SB_K, SB_V, SB_G = SB_BASE, SB_BASE + N_SB, SB_BASE + 2 * N_SB, SB_BASE + 3 * N_SB
DIFF_BASE = SB_BASE + 4 * N_SB
DIFF_Q, DIFF_K, DIFF_V, DIFF_G = (DIFF_BASE, DIFF_BASE + N_DIFF, DIFF_BASE + 2 * N_DIFF,
                                  DIFF_BASE + 3 * N_DIFF)

RET_HEADS_PER_STEP = 4
ATT_TQ = 1024
ATT_TK = 256
NEG = -1e30

V7X_VMEM_LIMIT = 56 * 1024 * 1024

F32 = jnp.float32
BF16 = jnp.bfloat16


def _alibi_slopes(n):
    def pow2(m):
        start = 2.0 ** (-8.0 / m)
        return [start ** (i + 1) for i in range(m)]
    if math.log2(n).is_integer():
        s = pow2(n)
    else:
        c = 2 ** math.floor(math.log2(n))
        s = pow2(c) + pow2(2 * c)[0::2][: n - c]
    return np.asarray(s, dtype=np.float32)


def _retention_log_decay(n):
    g = 1.0 - 2.0 ** (-5.0 - np.arange(n))
    return np.log(g).astype(np.float32)


def _proj_col_scale():
    s = np.ones((1, PROJ_W), np.float32)
    col = lambda blk: slice(blk * HEAD_DIM, (blk + 1) * HEAD_DIM)
    for h in range(N_RET):
        s[0, col(RET_K + h)] = HEAD_DIM ** -0.5
    for h in range(N_SB):
        s[0, col(SB_Q + h)] = LOG2E * HEAD_DIM ** -0.5
    for h in range(N_DIFF):
        s[0, col(DIFF_Q + h)] = LOG2E * (HEAD_DIM // 2) ** -0.5
    return s


def _silu(g):
    return g * (1.0 / (1.0 + jnp.exp(-g)))


def _dot_nt(a, b):
    return lax.dot_general(a, b, (((1,), (1,)), ((), ())), preferred_element_type=F32)


def _dot(a, b):
    return jnp.dot(a, b, preferred_element_type=F32)


def _neg_abs(x):
    bits = lax.bitcast_convert_type(x, jnp.uint32) | jnp.uint32(0x80000000)
    return lax.bitcast_convert_type(bits, F32)


def _lanes2(x):
    return jnp.concatenate([x, x], axis=1)


def _params(*sem, flags=None):
    return pltpu.CompilerParams(dimension_semantics=sem, vmem_limit_bytes=V7X_VMEM_LIMIT, flags=flags)


def _proj_kernel(x_ref, w_ref, s_ref, o_ref):
    w = w_ref[...].astype(BF16)
    o_ref[...] = (_dot(x_ref[...], w) * s_ref[...]).astype(o_ref.dtype)


def _proj(xb, w_all, layer, col_scale, *, tm=1024, tn=512):
    m, k = xb.shape
    n = w_all.shape[2]
    return pl.pallas_call(
        _proj_kernel,
        grid=(m // tm, n // tn),
        in_specs=[pl.BlockSpec((tm, k), lambda i, j: (i, 0)),
                  pl.BlockSpec((None, k, tn), lambda i, j: (layer, 0, j)),
                  pl.BlockSpec((1, tn), lambda i, j: (0, j))],
        out_specs=pl.BlockSpec((tm, tn), lambda i, j: (i, j)),
        out_shape=jax.ShapeDtypeStruct((m, n), BF16),
        compiler_params=_params("parallel", "parallel"),
        name="proj_matmul",
    )(xb, w_all, col_scale)


def _ret_kernel(lg_ref, q_ref, k_ref, v_ref, g_ref, gn_ref, o_ref,
                state_ref, inner_ref, qd_ref, kd_ref, cd_ref):
    hg = pl.program_id(1)
    nh = RET_HEADS_PER_STEP
    seq = q_ref.shape[0]
    c = CHUNK
    row = lax.broadcasted_iota(jnp.int32, (c, c), 0)
    col = lax.broadcasted_iota(jnp.int32, (c, c), 1)
    rel = (row - col).astype(F32)
    causal = row >= col
    rowf = row.astype(F32)
    for i in range(nh):
        lg = lg_ref[hg * nh + i]
        inner_ref[i] = jnp.where(causal, jnp.exp(lg * jnp.where(causal, rel, 0.0)), 0.0)
        qd_ref[i] = jnp.exp(lg * (rowf + 1.0))
        kd_ref[i] = jnp.exp(lg * (c - 1.0 - rowf))
        cd_ref[i] = jnp.exp(lg * (c + 0.0 * rowf))
    state_ref[...] = jnp.zeros_like(state_ref)

    def body(ci, carry):
        r0 = pl.multiple_of(ci * c, c)
        for i in range(nh):
            lanes = slice(i * HEAD_DIM, (i + 1) * HEAD_DIM)
            q = q_ref[pl.ds(r0, c), lanes]
            k = k_ref[pl.ds(r0, c), lanes]
            v = v_ref[pl.ds(r0, c), lanes]
            g = g_ref[pl.ds(r0, c), lanes].astype(F32)
            state = state_ref[i]
            scores = _dot_nt(q, k) * inner_ref[i]
            o = _dot(scores.astype(BF16), v)
            o = o + _dot(q, state.astype(BF16)) * qd_ref[i]
            kd = (k.astype(F32) * kd_ref[i]).astype(BF16)
            upd = lax.dot_general(kd, v, (((0,), (0,)), ((), ())), preferred_element_type=F32)
            state_ref[i] = state * cd_ref[i] + upd
            mu = jnp.mean(o, axis=-1, keepdims=True)
            oc = o - mu
            var = jnp.mean(oc * oc, axis=-1, keepdims=True)
            y = oc * lax.rsqrt(var + LN_EPS) * gn_ref[:, lanes] * _silu(g)
            o_ref[pl.ds(r0, c), lanes] = y.astype(o_ref.dtype)
        return carry

    lax.fori_loop(0, seq // c, body, 0, unroll=4)


def _retention(proj, gn_g, log_gamma, batch, seq):
    nh = RET_HEADS_PER_STEP
    w = nh * HEAD_DIM
    blk = lambda base: pl.BlockSpec((seq, w), lambda b, h: (b, base // nh + h))
    tab = pltpu.VMEM((nh, CHUNK, CHUNK), F32)
    return pl.pallas_call(
        _ret_kernel,
        grid=(batch, N_RET // nh),
        in_specs=[pl.BlockSpec(memory_space=pltpu.SMEM),
                  blk(RET_Q), blk(RET_K), blk(RET_V), blk(RET_G),
                  pl.BlockSpec((1, w), lambda b, h: (0, h))],
        out_specs=pl.BlockSpec((seq, w), lambda b, h: (b, h)),
        out_shape=jax.ShapeDtypeStruct((batch * seq, RET_W), BF16),
        scratch_shapes=[pltpu.VMEM((nh, HEAD_DIM, HEAD_DIM), F32), tab, tab, tab, tab],
        compiler_params=_params("parallel", "parallel"),
        name="retention",
    )(log_gamma, proj, proj, proj, proj, gn_g)


def _sb_kernel(q_ref, k_ref, v_ref, g_ref, o_ref, acc_ref, car_ref):
    seq = q_ref.shape[0]
    tq, tk = ATT_TQ, ATT_TK
    sub = tq // tk
    row = lax.broadcasted_iota(jnp.int32, (tk, tk), 0)
    col = lax.broadcasted_iota(jnp.int32, (tk, tk), 1)
    upper = jnp.where(row >= col, 1.0, 0.0).astype(BF16)
    row_in_blk = lax.broadcasted_iota(jnp.int32, (tq, tk), 0) & (tk - 1)
    strict = lax.broadcasted_iota(jnp.int32, (tq, tk), 1) < row_in_blk

    def stage1(z, mask):
        rows = z.shape[0]
        e = jnp.exp2(_neg_abs(z))
        nl = jnp.maximum(z, 0.0) + jnp.log2(1.0 + e)
        if mask is not None:
            nl = jnp.where(mask, nl, 0.0)
        rsum = jnp.broadcast_to(jnp.sum(nl, axis=-1, keepdims=True), (rows, HEAD_DIM))
        return nl.astype(BF16), rsum

    def stage2(z, hl, rsum, r0, mask, v_blocks):
        rows = z.shape[0]
        rs = pl.ds(r0, rows)
        car = car_ref[rs, :]
        a = jnp.exp2(z - (_dot(hl, upper) + _lanes2(car)))
        if mask is not None:
            a = jnp.where(mask, a, 0.0)
        a = a.astype(BF16)
        if len(v_blocks) == 1:
            upd = _dot(a, v_blocks[0])
        else:
            upd = jnp.concatenate(
                [_dot(a[i * tk:(i + 1) * tk], v) for i, v in enumerate(v_blocks)], axis=0)
        acc_ref[rs, :] += upd
        car_ref[rs, :] = car + rsum

    def qtile(qi, _):
        q0 = qi * tq
        acc_ref[...] = jnp.zeros_like(acc_ref)
        car_ref[...] = jnp.zeros_like(car_ref)

        def diag1(d):
            n = sub - d
            q = q_ref[pl.ds(q0 + d * tk, n * tk), :]
            z = jnp.concatenate(
                [_dot_nt(q[i * tk:(i + 1) * tk], k_ref[pl.ds(q0 + i * tk, tk), :]) for i in range(n)],
                axis=0)
            mask = strict if d == 0 else None
            return (z,) + stage1(z, mask)

        def diag2(d, staged):
            n = sub - d
            z, hl, rsum = staged
            mask = strict if d == 0 else None
            stage2(z, hl, rsum, d * tk, mask, [v_ref[pl.ds(q0 + i * tk, tk), :] for i in range(n)])

        staged = diag1(0)
        for d in range(sub):
            nxt = diag1(d + 1) if d + 1 < sub else None
            diag2(d, staged)
            staged = nxt

        def ktile(i, c):
            for j in range(sub):
                k0 = pl.multiple_of(((qi - i) * sub - 1 - j) * tk, tk)
                z = _dot_nt(q_ref[pl.ds(q0, tq), :], k_ref[pl.ds(k0, tk), :])
                hl, rsum = stage1(z, None)
                stage2(z, hl, rsum, 0, None, [v_ref[pl.ds(k0, tk), :]])
            return c

        lax.fori_loop(0, qi, ktile, 0)

        g = g_ref[pl.ds(q0, tq), :].astype(F32)
        o_ref[pl.ds(q0, tq), :] = (acc_ref[...] * _silu(g)).astype(o_ref.dtype)
        return 0

    for qi in range(seq // tq):
        qtile(qi, 0)


def _stick_breaking(proj, batch, seq):
    blk = lambda base: pl.BlockSpec((seq, HEAD_DIM), lambda b, h: (b, base + h))
    rows128 = pltpu.VMEM((ATT_TQ, HEAD_DIM), F32)
    return pl.pallas_call(
        _sb_kernel,
        grid=(batch, N_SB),
        in_specs=[blk(SB_Q), blk(SB_K), blk(SB_V), blk(SB_G)],
        out_specs=pl.BlockSpec((seq, HEAD_DIM), lambda b, h: (b, h)),
        out_shape=jax.ShapeDtypeStruct((batch * seq, SB_W), BF16),
        scratch_shapes=[rows128, rows128],
        compiler_params=_params("parallel", "parallel"),
        name="stick_breaking",
    )(proj, proj, proj, proj)


def _diff_kernel(slope_ref, lamv_ref, q_ref, k_ref, v_ref, g_ref, gain_ref, o_ref,
                 q1_ref, q2_ref, accl1_ref, accl2_ref, m1_ref, m2_ref, *, out_scale, lambda_init):
    h = pl.program_id(1)
    sl2 = slope_ref[h] * LOG2E
    seq = q_ref.shape[0]
    tq, tk = ATT_TQ, ATT_TK
    sub = tq // tk
    half = HEAD_DIM // 2
    causal = (lax.broadcasted_iota(jnp.int32, (tk, tk), 1)
              <= lax.broadcasted_iota(jnp.int32, (tk, tk), 0))
    ones = jnp.ones((tk, HEAD_DIM), BF16)
    frame_shift = sl2 * tk
    lane_q = lax.broadcasted_iota(jnp.int32, (tq, HEAD_DIM), 1)
    lane_k = lax.broadcasted_iota(jnp.int32, (tk, HEAD_DIM), 1)
    s_full = jnp.full((tq, HEAD_DIM), sl2, F32)
    s_hi = s_full.astype(BF16).astype(F32)
    s_mid = (s_full - s_hi).astype(BF16).astype(F32)
    s_lo = (s_full - s_hi - s_mid).astype(BF16).astype(F32)

    def slope_feats(base):
        f = jnp.where(lane_q == base, s_hi,
                      jnp.where(lane_q == base + 1, s_mid, jnp.where(lane_q == base + 2, s_lo, 0.0)))
        return f.astype(BF16)

    key_pos = lax.broadcasted_iota(jnp.int32, (tk, HEAD_DIM), 0).astype(F32)

    def pos_feats(base):
        f = jnp.where(lane_k >= base, jnp.where(lane_k < base + 3, key_pos, 0.0), 0.0)
        return f.astype(BF16)

    def keep(lane, first):
        return jnp.where((lane < half) if first else (lane >= half), 1.0, 0.0).astype(BF16)

    q_feats = (slope_feats(half), slope_feats(0))
    k_feats = (pos_feats(half), pos_feats(0))
    q_keep = (keep(lane_q, True), keep(lane_q, False))
    k_keep = (keep(lane_k, True), keep(lane_k, False))

    lv = lamv_ref[...]
    lam = (jnp.exp(jnp.sum(lv[0:1] * lv[1:2], axis=-1, keepdims=True))
           - jnp.exp(jnp.sum(lv[2:3] * lv[3:4], axis=-1, keepdims=True)) + lambda_init)

    maps = ((q1_ref, m1_ref, accl1_ref), (q2_ref, m2_ref, accl2_ref))

    def block(r0, rows, k_starts, mask):
        rs = pl.ds(r0, rows)
        vs = [jnp.concatenate([v_ref[pl.ds(k0, tk), :], ones], axis=1) for k0 in k_starts]
        for mi, (qm_ref, m_ref, accl_ref) in enumerate(maps):
            parts = []
            for i, k0 in enumerate(k_starts):
                k = k_ref[pl.ds(k0, tk), :] * k_keep[mi] + k_feats[mi]
                rows_i = rows if len(k_starts) == 1 else tk
                parts.append(_dot_nt(qm_ref[pl.ds(r0 + i * tk, rows_i), :], k))
            y = parts[0] if len(parts) == 1 else jnp.concatenate(parts, axis=0)
            if mask is not None:
                y = jnp.where(mask, y, NEG)
            m_blk = jnp.broadcast_to(jnp.max(y, axis=-1, keepdims=True), (rows, HEAD_DIM))
            m_old = m_ref[rs, :] - frame_shift
            m_new = jnp.maximum(m_old, m_blk)
            p = jnp.exp2(y - _lanes2(m_new)).astype(BF16)
            alpha = jnp.exp2(m_old - m_new)
            if len(vs) == 1:
                upd = _dot(p, vs[0])
            else:
                upd = jnp.concatenate(
                    [_dot(p[i * tk:(i + 1) * tk], v) for i, v in enumerate(vs)], axis=0)
            accl_ref[rs, :] = _lanes2(alpha) * accl_ref[rs, :] + upd
            m_ref[rs, :] = m_new

    def qtile(qi, _):
        q0 = qi * tq
        q = q_ref[pl.ds(q0, tq), :]
        q1_ref[...] = q * q_keep[0] + q_feats[0]
        q2_ref[...] = q * q_keep[1] + q_feats[1]
        for _, m_ref, accl_ref in maps:
            m_ref[...] = jnp.full_like(m_ref, NEG)
            accl_ref[...] = jnp.zeros_like(accl_ref)

        def ktile(i, c):
            for j in range(sub):
                block(0, tq, [pl.multiple_of((sub * i + j) * tk, tk)], None)
            return c

        lax.fori_loop(0, qi, ktile, 0)

        for jd in range(sub):
            k0 = q0 + jd * tk
            block(jd * tk, tk, [k0], causal)
            if jd + 1 < sub:
                block((jd + 1) * tk, tq - (jd + 1) * tk, [k0], None)

        o1 = accl1_ref[:, :HEAD_DIM] * (1.0 / accl1_ref[:, HEAD_DIM:])
        o2 = accl2_ref[:, :HEAD_DIM] * (1.0 / accl2_ref[:, HEAD_DIM:])
        o = o1 - lam * o2
        o = o * lax.rsqrt(jnp.mean(o * o, axis=-1, keepdims=True) + LN_EPS)
        g = g_ref[pl.ds(q0, tq), :].astype(F32)
        o_ref[pl.ds(q0, tq), :] = (o * (gain_ref[...] * out_scale) * _silu(g)).astype(o_ref.dtype)
        return 0

    for qi in range(seq // tq):
        qtile(qi, 0)


def _diff_attention(proj, gain, lamv, slopes, batch, seq, *, lambda_init):
    blk = lambda base: pl.BlockSpec((seq, HEAD_DIM), lambda b, h: (b, base + h))
    qm = pltpu.VMEM((ATT_TQ, HEAD_DIM), BF16)
    accl = pltpu.VMEM((ATT_TQ, 2 * HEAD_DIM), F32)
    stat = pltpu.VMEM((ATT_TQ, HEAD_DIM), F32)
    return pl.pallas_call(
        functools.partial(_diff_kernel, out_scale=1.0 - lambda_init, lambda_init=lambda_init),
        grid=(batch, N_DIFF),
        in_specs=[pl.BlockSpec(memory_space=pltpu.SMEM),
                  pl.BlockSpec((4, HEAD_DIM // 2), lambda b, h: (0, 0)),
                  blk(DIFF_Q), blk(DIFF_K), blk(DIFF_V), blk(DIFF_G),
                  pl.BlockSpec((1, HEAD_DIM), lambda b, h: (0, h))],
        out_specs=pl.BlockSpec((seq, HEAD_DIM), lambda b, h: (b, h)),
        out_shape=jax.ShapeDtypeStruct((batch * seq, DIFF_W), BF16),
        scratch_shapes=[qm, qm, accl, accl, stat, stat],
        compiler_params=_params("parallel", "parallel"),
        name="diff_attention",
    )(slopes, lamv, proj, proj, proj, proj, gain)


def _outproj_kernel(ro_ref, so_ref, do_ref, w_ref, x_ref, o_ref, *, alpha):
    y = _dot(ro_ref[...], w_ref[0:RET_W, :])
    y = y + _dot(so_ref[...], w_ref[RET_W:RET_W + SB_W, :])
    y = y + _dot(do_ref[...], w_ref[RET_W + SB_W:, :])
    o_ref[...] = alpha * x_ref[...] + y


def _outproj(ro, so, do, w_all, layer, x, *, alpha, tm=1024, tn=1024):
    m = ro.shape[0]
    k, n = w_all.shape[1], w_all.shape[2]
    row_blk = lambda w: pl.BlockSpec((tm, w), lambda i, j: (i, 0))
    return pl.pallas_call(
        functools.partial(_outproj_kernel, alpha=alpha),
        grid=(m // tm, n // tn),
        in_specs=[row_blk(RET_W), row_blk(SB_W), row_blk(DIFF_W),
                  pl.BlockSpec((None, k, tn), lambda i, j: (layer, 0, j)),
                  pl.BlockSpec((tm, tn), lambda i, j: (i, j))],
        out_specs=pl.BlockSpec((tm, tn), lambda i, j: (i, j)),
        out_shape=jax.ShapeDtypeStruct((m, n), F32),
        compiler_params=_params("parallel", "parallel"),
        name="out_proj_residual",
    )(ro, so, do, w_all, x)


def _ln_kernel(z_ref, g_ref, b_ref, o_ref, ob_ref):
    z = z_ref[...]
    mu = jnp.mean(z, axis=-1, keepdims=True)
    zc = z - mu
    var = jnp.mean(zc * zc, axis=-1, keepdims=True)
    y = zc * lax.rsqrt(var + LN_EPS) * g_ref[...] + b_ref[...]
    o_ref[...] = y
    ob_ref[...] = y.astype(ob_ref.dtype)


def _layer_norm(z, g, b, *, tr=256):
    m, d = z.shape
    return pl.pallas_call(
        _ln_kernel,
        grid=(m // tr,),
        in_specs=[pl.BlockSpec((tr, d), lambda i: (i, 0)),
                  pl.BlockSpec((1, d), lambda i: (0, 0)),
                  pl.BlockSpec((1, d), lambda i: (0, 0))],
        out_specs=[pl.BlockSpec((tr, d), lambda i: (i, 0)),
                   pl.BlockSpec((tr, d), lambda i: (i, 0))],
        out_shape=[jax.ShapeDtypeStruct((m, d), F32), jax.ShapeDtypeStruct((m, d), BF16)],
        compiler_params=_params("parallel"),
        name="layer_norm",
    )(z, g, b)


def kernel(x, w_in, w_out, ln_g, ln_b, ret_gn_g, diff_ln_g, lam_q1, lam_k1, lam_q2, lam_k2):
    batch, seq, d_model = x.shape
    depth = w_in.shape[0]
    alpha = (2.0 * depth) ** 0.25
    log_gamma = jnp.asarray(_retention_log_decay(N_RET))
    slopes = jnp.asarray(_alibi_slopes(N_DIFF))
    col_scale = jnp.asarray(_proj_col_scale())
    w_out_b = w_out.astype(BF16)

    xf = x.reshape(batch * seq, d_model)
    xb = xf.astype(BF16)
    for layer in range(depth):
        lambda_init = 0.8 - 0.6 * math.exp(-0.3 * layer)
        proj = _proj(xb, w_in, layer, col_scale)
        ro = _retention(proj, ret_gn_g[layer][None, :], log_gamma, batch, seq)
        so = _stick_breaking(proj, batch, seq)
        lamv = jnp.stack([lam_q1[layer], lam_k1[layer], lam_q2[layer], lam_k2[layer]]).astype(F32)
        do = _diff_attention(proj, diff_ln_g[layer][None, :], lamv, slopes, batch, seq,
                             lambda_init=lambda_init)
        z = _outproj(ro, so, do, w_out_b, layer, xf, alpha=alpha)
        xf, xb = _layer_norm(z, ln_g[layer][None, :], ln_b[layer][None, :])
    return xf.reshape(batch, seq, d_model)
```

```python
import functools
import math

import jax
import jax.numpy as jnp
import numpy as np
from jax import lax
from jax.experimental import pallas as pl
from jax.experimental.pallas import tpu as pltpu

HEAD_DIM = 128
N_RET = 12
N_SB = 10
N_DIFF = 10
RET_W = N_RET * HEAD_DIM
SB_W = N_SB * HEAD_DIM
DIFF_W = N_DIFF * HEAD_DIM
PROJ_W = 4 * (RET_W + SB_W + DIFF_W)
CHUNK = 128
LN_EPS = 1e-5
LOG2E = math.log2(math.e)

RET_Q, RET_K, RET_V, RET_G = 0, N_RET, 2 * N_RET, 3 * N_RET
SB_BASE = 4 * N_RET
SB_Q, SB_K, SB_V, SB_G = SB_BASE, SB_BASE + N_SB, SB_BASE + 2 * N_SB, SB_BASE + 3 * N_SB
DIFF_BASE = SB_BASE + 4 * N_SB
DIFF_Q, DIFF_K, DIFF_V, DIFF_G = (DIFF_BASE, DIFF_BASE + N_DIFF, DIFF_BASE + 2 * N_DIFF,
                                  DIFF_BASE + 3 * N_DIFF)

RET_HEADS_PER_STEP = 4
ATT_TQ = 1024
ATT_TK = 256
NEG = -1e30

V7X_VMEM_LIMIT = 56 * 1024 * 1024

F32 = jnp.float32
BF16 = jnp.bfloat16


def _alibi_slopes(n):
    def pow2(m):
        start = 2.0 ** (-8.0 / m)
        return [start ** (i + 1) for i in range(m)]
    if math.log2(n).is_integer():
        s = pow2(n)
    else:
        c = 2 ** math.floor(math.log2(n))
        s = pow2(c) + pow2(2 * c)[0::2][: n - c]
    return np.asarray(s, dtype=np.float32)


def _retention_log_decay(n):
    g = 1.0 - 2.0 ** (-5.0 - np.arange(n))
    return np.log(g).astype(np.float32)


def _proj_col_scale():
    s = np.ones((1, PROJ_W), np.float32)
    col = lambda blk: slice(blk * HEAD_DIM, (blk + 1) * HEAD_DIM)
    for h in range(N_RET):
        s[0, col(RET_K + h)] = HEAD_DIM ** -0.5
    for h in range(N_SB):
        s[0, col(SB_Q + h)] = LOG2E * HEAD_DIM ** -0.5
    for h in range(N_DIFF):
        s[0, col(DIFF_Q + h)] = LOG2E * (HEAD_DIM // 2) ** -0.5
    return s


def _silu(g):
    return g * (1.0 / (1.0 + jnp.exp(-g)))


def _dot_nt(a, b):
    return lax.dot_general(a, b, (((1,), (1,)), ((), ())), preferred_element_type=F32)


def _dot(a, b):
    return jnp.dot(a, b, preferred_element_type=F32)


def _neg_abs(x):
    bits = lax.bitcast_convert_type(x, jnp.uint32) | jnp.uint32(0x80000000)
    return lax.bitcast_convert_type(bits, F32)


def _lanes2(x):
    return jnp.concatenate([x, x], axis=1)


def _params(*sem, flags=None):
    return pltpu.CompilerParams(dimension_semantics=sem, vmem_limit_bytes=V7X_VMEM_LIMIT, flags=flags)


def _proj_kernel(x_ref, w_ref, s_ref, o_ref):
    w = w_ref[...].astype(BF16)
    o_ref[...] = (_dot(x_ref[...], w) * s_ref[...]).astype(o_ref.dtype)


def _proj(xb, w_all, layer, col_scale, *, tm=1024, tn=512):
    m, k = xb.shape
    n = w_all.shape[2]
    return pl.pallas_call(
        _proj_kernel,
        grid=(m // tm, n // tn),
        in_specs=[pl.BlockSpec((tm, k), lambda i, j: (i, 0)),
                  pl.BlockSpec((None, k, tn), lambda i, j: (layer, 0, j)),
                  pl.BlockSpec((1, tn), lambda i, j: (0, j))],
        out_specs=pl.BlockSpec((tm, tn), lambda i, j: (i, j)),
        out_shape=jax.ShapeDtypeStruct((m, n), BF16),
        compiler_params=_params("parallel", "parallel"),
        name="proj_matmul",
    )(xb, w_all, col_scale)


def _ret_kernel(lg_ref, q_ref, k_ref, v_ref, g_ref, gn_ref, o_ref,
                state_ref, inner_ref, qd_ref, kd_ref, cd_ref):
    hg = pl.program_id(1)
    nh = RET_HEADS_PER_STEP
    seq = q_ref.shape[0]
    c = CHUNK
    row = lax.broadcasted_iota(jnp.int32, (c, c), 0)
    col = lax.broadcasted_iota(jnp.int32, (c, c), 1)
    rel = (row - col).astype(F32)
    causal = row >= col
    rowf = row.astype(F32)
    for i in range(nh):
        lg = lg_ref[hg * nh + i]
        inner_ref[i] = jnp.where(causal, jnp.exp(lg * jnp.where(causal, rel, 0.0)), 0.0)
        qd_ref[i] = jnp.exp(lg * (rowf + 1.0))
        kd_ref[i] = jnp.exp(lg * (c - 1.0 - col.astype(F32)))
        cd_ref[i] = jnp.exp(lg * (c + 0.0 * rowf))
    state_ref[...] = jnp.zeros_like(state_ref)

    def body(ci, carry):
        r0 = pl.multiple_of(ci * c, c)
        for i in range(nh):
            lanes = slice(i * HEAD_DIM, (i + 1) * HEAD_DIM)
            q = q_ref[pl.ds(r0, c), lanes]
            k = k_ref[pl.ds(r0, c), lanes]
            v = v_ref[pl.ds(r0, c), lanes]
            g = g_ref[pl.ds(r0, c), lanes].astype(F32)
            state = state_ref[i]
            kt = k.astype(F32).T
            qk_qs = _dot(q, jnp.concatenate([kt.astype(BF16), state.astype(BF16)], axis=1))
            scores = (qk_qs[:, :c] * inner_ref[i]).astype(BF16)
            kd_t = (kt * kd_ref[i]).astype(BF16)
            sv_kv = _dot(jnp.concatenate([scores, kd_t], axis=0), v)
            o = sv_kv[:c] + qk_qs[:, c:] * qd_ref[i]
            state_ref[i] = state * cd_ref[i] + sv_kv[c:]
            mu = jnp.mean(o, axis=-1, keepdims=True)
            oc = o - mu
            var = jnp.mean(oc * oc, axis=-1, keepdims=True)
            y = oc * lax.rsqrt(var + LN_EPS) * gn_ref[:, lanes] * _silu(g)
            o_ref[pl.ds(r0, c), lanes] = y.astype(o_ref.dtype)
        return carry

    lax.fori_loop(0, seq // c, body, 0, unroll=4)


def _retention(proj, gn_g, log_gamma, batch, seq):
    nh = RET_HEADS_PER_STEP
    w = nh * HEAD_DIM
    blk = lambda base: pl.BlockSpec((seq, w), lambda b, h: (b, base // nh + h))
    tab = pltpu.VMEM((nh, CHUNK, CHUNK), F32)
    return pl.pallas_call(
        _ret_kernel,
        grid=(batch, N_RET // nh),
        in_specs=[pl.BlockSpec(memory_space=pltpu.SMEM),
                  blk(RET_Q), blk(RET_K), blk(RET_V), blk(RET_G),
                  pl.BlockSpec((1, w), lambda b, h: (0, h))],
        out_specs=pl.BlockSpec((seq, w), lambda b, h: (b, h)),
        out_shape=jax.ShapeDtypeStruct((batch * seq, RET_W), BF16),
        scratch_shapes=[pltpu.VMEM((nh, HEAD_DIM, HEAD_DIM), F32), tab, tab, tab, tab],
        compiler_params=_params("parallel", "parallel"),
        name="retention",
    )(log_gamma, proj, proj, proj, proj, gn_g)


def _sb_parts(q_ref, k_ref, v_ref, g_ref, o_ref, acc_ref, car_ref):
    tq, tk = ATT_TQ, ATT_TK
    sub = tq // tk
    row = lax.broadcasted_iota(jnp.int32, (tk, tk), 0)
    col = lax.broadcasted_iota(jnp.int32, (tk, tk), 1)
    upper = jnp.where(row >= col, 1.0, 0.0).astype(BF16)
    row_in_blk = lax.broadcasted_iota(jnp.int32, (tq, tk), 0) & (tk - 1)
    strict = lax.broadcasted_iota(jnp.int32, (tq, tk), 1) < row_in_blk

    def stage1(z, mask):
        rows = z.shape[0]
        e = jnp.exp2(_neg_abs(z))
        nl = jnp.maximum(z, 0.0) + jnp.log2(1.0 + e)
        if mask is not None:
            nl = jnp.where(mask, nl, 0.0)
        rsum = jnp.broadcast_to(jnp.sum(nl, axis=-1, keepdims=True), (rows, HEAD_DIM))
        return nl.astype(BF16), rsum

    def stage2(z, hl, rsum, r0, mask, v_blocks):
        rows = z.shape[0]
        rs = pl.ds(r0, rows)
        car = car_ref[rs, :]
        a = jnp.exp2(z - (_dot(hl, upper) + _lanes2(car)))
        if mask is not None:
            a = jnp.where(mask, a, 0.0)
        a = a.astype(BF16)
        if len(v_blocks) == 1:
            upd = _dot(a, v_blocks[0])
        else:
            upd = jnp.concatenate(
                [_dot(a[i * tk:(i + 1) * tk], v) for i, v in enumerate(v_blocks)], axis=0)
        acc_ref[rs, :] += upd
        car_ref[rs, :] = car + rsum

    def init(q0):
        acc_ref[...] = jnp.zeros_like(acc_ref)
        car_ref[...] = jnp.zeros_like(car_ref)

    def diag(q0):
        def diag1(d):
            n = sub - d
            q = q_ref[pl.ds(q0 + d * tk, n * tk), :]
            z = jnp.concatenate(
                [_dot_nt(q[i * tk:(i + 1) * tk], k_ref[pl.ds(q0 + i * tk, tk), :]) for i in range(n)],
                axis=0)
            mask = strict if d == 0 else None
            return (z,) + stage1(z, mask)

        def diag2(d, staged):
            n = sub - d
            z, hl, rsum = staged
            mask = strict if d == 0 else None
            stage2(z, hl, rsum, d * tk, mask, [v_ref[pl.ds(q0 + i * tk, tk), :] for i in range(n)])

        staged = diag1(0)
        for d in range(sub):
            nxt = diag1(d + 1) if d + 1 < sub else None
            diag2(d, staged)
            staged = nxt

    def ktile(q0, t):
        for j in range(sub):
            k0 = pl.multiple_of(((t + 1) * sub - 1 - j) * tk, tk)
            z = _dot_nt(q_ref[pl.ds(q0, tq), :], k_ref[pl.ds(k0, tk), :])
            hl, rsum = stage1(z, None)
            stage2(z, hl, rsum, 0, None, [v_ref[pl.ds(k0, tk), :]])

    def final(q0):
        g = g_ref[pl.ds(q0, tq), :].astype(F32)
        o_ref[pl.ds(q0, tq), :] = (acc_ref[...] * _silu(g)).astype(o_ref.dtype)

    return init, diag, ktile, final


def _diff_parts(slope_ref, lamv_ref, q_ref, k_ref, v_ref, g_ref, gain_ref, o_ref,
                q1_ref, q2_ref, accl1_ref, accl2_ref, m1_ref, m2_ref, *, out_scale, lambda_init):
    h = pl.program_id(1)
    sl2 = slope_ref[h] * LOG2E
    tq, tk = ATT_TQ, ATT_TK
    sub = tq // tk
    half = HEAD_DIM // 2
    causal = (lax.broadcasted_iota(jnp.int32, (tk, tk), 1)
              <= lax.broadcasted_iota(jnp.int32, (tk, tk), 0))
    ones = jnp.ones((tk, HEAD_DIM), BF16)
    frame_shift = sl2 * tk
    lane_q = lax.broadcasted_iota(jnp.int32, (tq, HEAD_DIM), 1)
    lane_k = lax.broadcasted_iota(jnp.int32, (tk, HEAD_DIM), 1)
    s_full = jnp.full((tq, HEAD_DIM), sl2, F32)
    s_hi = s_full.astype(BF16).astype(F32)
    s_mid = (s_full - s_hi).astype(BF16).astype(F32)
    s_lo = (s_full - s_hi - s_mid).astype(BF16).astype(F32)

    def slope_feats(base):
        f = jnp.where(lane_q == base, s_hi,
                      jnp.where(lane_q == base + 1, s_mid, jnp.where(lane_q == base + 2, s_lo, 0.0)))
        return f.astype(BF16)

    key_pos = lax.broadcasted_iota(jnp.int32, (tk, HEAD_DIM), 0).astype(F32)

    def pos_feats(base):
        f = jnp.where(lane_k >= base, jnp.where(lane_k < base + 3, key_pos, 0.0), 0.0)
        return f.astype(BF16)

    def keep(lane, first):
        return jnp.where((lane < half) if first else (lane >= half), 1.0, 0.0).astype(BF16)

    q_feats = (slope_feats(half), slope_feats(0))
    k_feats = (pos_feats(half), pos_feats(0))
    q_keep = (keep(lane_q, True), keep(lane_q, False))
    k_keep = (keep(lane_k, True), keep(lane_k, False))

    lv = lamv_ref[...]
    lam = (jnp.exp(jnp.sum(lv[0:1] * lv[1:2], axis=-1, keepdims=True))
           - jnp.exp(jnp.sum(lv[2:3] * lv[3:4], axis=-1, keepdims=True)) + lambda_init)

    maps = ((q1_ref, m1_ref, accl1_ref), (q2_ref, m2_ref, accl2_ref))

    def block(r0, rows, k_starts, mask):
        rs = pl.ds(r0, rows)
        vs = [jnp.concatenate([v_ref[pl.ds(k0, tk), :], ones], axis=1) for k0 in k_starts]
        for mi, (qm_ref, m_ref, accl_ref) in enumerate(maps):
            parts = []
            for i, k0 in enumerate(k_starts):
                k = k_ref[pl.ds(k0, tk), :] * k_keep[mi] + k_feats[mi]
                rows_i = rows if len(k_starts) == 1 else tk
                parts.append(_dot_nt(qm_ref[pl.ds(r0 + i * tk, rows_i), :], k))
            y = parts[0] if len(parts) == 1 else jnp.concatenate(parts, axis=0)
            if mask is not None:
                y = jnp.where(mask, y, NEG)
            m_blk = jnp.broadcast_to(jnp.max(y, axis=-1, keepdims=True), (rows, HEAD_DIM))
            m_old = m_ref[rs, :] - frame_shift
            m_new = jnp.maximum(m_old, m_blk)
            p = jnp.exp2(y - _lanes2(m_new)).astype(BF16)
            alpha = jnp.exp2(m_old - m_new)
            if len(vs) == 1:
                upd = _dot(p, vs[0])
            else:
                upd = jnp.concatenate(
                    [_dot(p[i * tk:(i + 1) * tk], v) for i, v in enumerate(vs)], axis=0)
            accl_ref[rs, :] = _lanes2(alpha) * accl_ref[rs, :] + upd
            m_ref[rs, :] = m_new

    def init(q0):
        q = q_ref[pl.ds(q0, tq), :]
        q1_ref[...] = q * q_keep[0] + q_feats[0]
        q2_ref[...] = q * q_keep[1] + q_feats[1]
        for _, m_ref, accl_ref in maps:
            m_ref[...] = jnp.full_like(m_ref, NEG)
            accl_ref[...] = jnp.zeros_like(accl_ref)

    def ktile(q0, t):
        for j in range(sub):
            block(0, tq, [pl.multiple_of((sub * t + j) * tk, tk)], None)

    def diag(q0):
        for jd in range(sub):
            k0 = q0 + jd * tk
            block(jd * tk, tk, [k0], causal)
            if jd + 1 < sub:
                block((jd + 1) * tk, tq - (jd + 1) * tk, [k0], None)

    def final(q0):
        o1 = accl1_ref[:, :HEAD_DIM] * (1.0 / accl1_ref[:, HEAD_DIM:])
        o2 = accl2_ref[:, :HEAD_DIM] * (1.0 / accl2_ref[:, HEAD_DIM:])
        o = o1 - lam * o2
        o = o * lax.rsqrt(jnp.mean(o * o, axis=-1, keepdims=True) + LN_EPS)
        g = g_ref[pl.ds(q0, tq), :].astype(F32)
        o_ref[pl.ds(q0, tq), :] = (o * (gain_ref[...] * out_scale) * _silu(g)).astype(o_ref.dtype)

    return init, diag, ktile, final


def _attn_kernel(slope_ref, lamv_ref, sq_ref, sk_ref, sv_ref, sg_ref, dq_ref, dk_ref, dv_ref, dg_ref,
                 gain_ref, so_ref, do_ref, acc_ref, car_ref,
                 q1_ref, q2_ref, accl1_ref, accl2_ref, m1_ref, m2_ref, *, out_scale, lambda_init):
    seq = sq_ref.shape[0]
    tq = ATT_TQ
    sb_init, sb_diag, sb_ktile, sb_final = _sb_parts(sq_ref, sk_ref, sv_ref, sg_ref, so_ref, acc_ref, car_ref)
    df_init, df_diag, df_ktile, df_final = _diff_parts(
        slope_ref, lamv_ref, dq_ref, dk_ref, dv_ref, dg_ref, gain_ref, do_ref,
        q1_ref, q2_ref, accl1_ref, accl2_ref, m1_ref, m2_ref,
        out_scale=out_scale, lambda_init=lambda_init)
    for qi in range(seq // tq):
        q0 = qi * tq
        sb_init(q0)
        df_init(q0)
        sb_diag(q0)

        def trip(i, c, q0=q0, qi=qi):
            sb_ktile(q0, qi - 1 - i)
            df_ktile(q0, i)
            return c

        lax.fori_loop(0, qi, trip, 0)
        df_diag(q0)
        sb_final(q0)
        df_final(q0)


def _attention(proj, gain, lamv, slopes, batch, seq, *, lambda_init):
    assert N_SB == N_DIFF
    blk = lambda base: pl.BlockSpec((seq, HEAD_DIM), lambda b, h: (b, base + h))
    out_blk = pl.BlockSpec((seq, HEAD_DIM), lambda b, h: (b, h))
    rows128 = pltpu.VMEM((ATT_TQ, HEAD_DIM), F32)
    qm = pltpu.VMEM((ATT_TQ, HEAD_DIM), BF16)
    accl = pltpu.VMEM((ATT_TQ, 2 * HEAD_DIM), F32)
    return pl.pallas_call(
        functools.partial(_attn_kernel, out_scale=1.0 - lambda_init, lambda_init=lambda_init),
        grid=(batch, N_SB),
        in_specs=[pl.BlockSpec(memory_space=pltpu.SMEM),
                  pl.BlockSpec((4, HEAD_DIM // 2), lambda b, h: (0, 0)),
                  blk(SB_Q), blk(SB_K), blk(SB_V), blk(SB_G),
                  blk(DIFF_Q), blk(DIFF_K), blk(DIFF_V), blk(DIFF_G),
                  pl.BlockSpec((1, HEAD_DIM), lambda b, h: (0, h))],
        out_specs=[out_blk, out_blk],
        out_shape=[jax.ShapeDtypeStruct((batch * seq, SB_W), BF16),
                   jax.ShapeDtypeStruct((batch * seq, DIFF_W), BF16)],
        scratch_shapes=[rows128, rows128, qm, qm, accl, accl, rows128, rows128],
        compiler_params=_params("parallel", "parallel"),
        name="sb_diff_attention",
    )(slopes, lamv, proj, proj, proj, proj, proj, proj, proj, proj, gain)


def _outproj_kernel(ro_ref, so_ref, do_ref, w_ref, x_ref, o_ref, *, alpha):
    y = _dot(ro_ref[...], w_ref[0:RET_W, :])
    y = y + _dot(so_ref[...], w_ref[RET_W:RET_W + SB_W, :])
    y = y + _dot(do_ref[...], w_ref[RET_W + SB_W:, :])
    o_ref[...] = alpha * x_ref[...] + y


def _outproj(ro, so, do, w_all, layer, x, *, alpha, tm=1024, tn=1024):
    m = ro.shape[0]
    k, n = w_all.shape[1], w_all.shape[2]
    row_blk = lambda w: pl.BlockSpec((tm, w), lambda i, j: (i, 0))
    return pl.pallas_call(
        functools.partial(_outproj_kernel, alpha=alpha),
        grid=(m // tm, n // tn),
        in_specs=[row_blk(RET_W), row_blk(SB_W), row_blk(DIFF_W),
                  pl.BlockSpec((None, k, tn), lambda i, j: (layer, 0, j)),
                  pl.BlockSpec((tm, tn), lambda i, j: (i, j))],
        out_specs=pl.BlockSpec((tm, tn), lambda i, j: (i, j)),
        out_shape=jax.ShapeDtypeStruct((m, n), F32),
        compiler_params=_params("parallel", "parallel"),
        name="out_proj_residual",
    )(ro, so, do, w_all, x)


def _ln_kernel(z_ref, g_ref, b_ref, o_ref, ob_ref):
    z = z_ref[...]
    mu = jnp.mean(z, axis=-1, keepdims=True)
    zc = z - mu
    var = jnp.mean(zc * zc, axis=-1, keepdims=True)
    y = zc * lax.rsqrt(var + LN_EPS) * g_ref[...] + b_ref[...]
    o_ref[...] = y
    ob_ref[...] = y.astype(ob_ref.dtype)


def _layer_norm(z, g, b, *, tr=256):
    m, d = z.shape
    return pl.pallas_call(
        _ln_kernel,
        grid=(m // tr,),
        in_specs=[pl.BlockSpec((tr, d), lambda i: (i, 0)),
                  pl.BlockSpec((1, d), lambda i: (0, 0)),
                  pl.BlockSpec((1, d), lambda i: (0, 0))],
        out_specs=[pl.BlockSpec((tr, d), lambda i: (i, 0)),
                   pl.BlockSpec((tr, d), lambda i: (i, 0))],
        out_shape=[jax.ShapeDtypeStruct((m, d), F32), jax.ShapeDtypeStruct((m, d), BF16)],
        compiler_params=_params("parallel"),
        name="layer_norm",
    )(z, g, b)


def kernel(x, w_in, w_out, ln_g, ln_b, ret_gn_g, diff_ln_g, lam_q1, lam_k1, lam_q2, lam_k2):
    batch, seq, d_model = x.shape
    depth = w_in.shape[0]
    alpha = (2.0 * depth) ** 0.25
    log_gamma = jnp.asarray(_retention_log_decay(N_RET))
    slopes = jnp.asarray(_alibi_slopes(N_DIFF))
    col_scale = jnp.asarray(_proj_col_scale())
    w_out_b = w_out.astype(BF16)

    xf = x.reshape(batch * seq, d_model)
    xb = xf.astype(BF16)
    for layer in range(depth):
        lambda_init = 0.8 - 0.6 * math.exp(-0.3 * layer)
        proj = _proj(xb, w_in, layer, col_scale)
        ro = _retention(proj, ret_gn_g[layer][None, :], log_gamma, batch, seq)
        lamv = jnp.stack([lam_q1[layer], lam_k1[layer], lam_q2[layer], lam_k2[layer]]).astype(F32)
        so, do = _attention(proj, diff_ln_g[layer][None, :], lamv, slopes, batch, seq,
                            lambda_init=lambda_init)
        z = _outproj(ro, so, do, w_out_b, layer, xf, alpha=alpha)
        xf, xb = _layer_norm(z, ln_g[layer][None, :], ln_b[layer][None, :])
    return xf.reshape(batch, seq, d_model)
```

```python
import functools
import math

import jax
import jax.numpy as jnp
import numpy as np
from jax import lax
from jax.experimental import pallas as pl
from jax.experimental.pallas import tpu as pltpu

HEAD_DIM = 128
N_RET = 12
N_SB = 10
N_DIFF = 10
RET_W = N_RET * HEAD_DIM
SB_W = N_SB * HEAD_DIM
DIFF_W = N_DIFF * HEAD_DIM
PROJ_W = 4 * (RET_W + SB_W + DIFF_W)
CHUNK = 128
LN_EPS = 1e-5
LOG2E = math.log2(math.e)

RET_Q, RET_K, RET_V, RET_G = 0, N_RET, 2 * N_RET, 3 * N_RET
SB_BASE = 4 * N_RET
SB_Q, SB_K, SB_V, SB_G = SB_BASE, SB_BASE + N_SB, SB_BASE + 2 * N_SB, SB_BASE + 3 * N_SB
DIFF_BASE = SB_BASE + 4 * N_SB
DIFF_Q, DIFF_K, DIFF_V, DIFF_G = (DIFF_BASE, DIFF_BASE + N_DIFF, DIFF_BASE + 2 * N_DIFF,
                                  DIFF_BASE + 3 * N_DIFF)

RET_HEADS_PER_STEP = 4
ATT_TQ = 1024
ATT_TK = 256
NEG = -1e30

V7X_VMEM_LIMIT = 56 * 1024 * 1024

F32 = jnp.float32
BF16 = jnp.bfloat16


def _alibi_slopes(n):
    def pow2(m):
        start = 2.0 ** (-8.0 / m)
        return [start ** (i + 1) for i in range(m)]
    if math.log2(n).is_integer():
        s = pow2(n)
    else:
        c = 2 ** math.floor(math.log2(n))
        s = pow2(c) + pow2(2 * c)[0::2][: n - c]
    return np.asarray(s, dtype=np.float32)


def _retention_log_decay(n):
    g = 1.0 - 2.0 ** (-5.0 - np.arange(n))
    return np.log(g).astype(np.float32)


def _proj_col_scale():
    s = np.ones((1, PROJ_W), np.float32)
    col = lambda blk: slice(blk * HEAD_DIM, (blk + 1) * HEAD_DIM)
    for h in range(N_RET):
        s[0, col(RET_K + h)] = HEAD_DIM ** -0.5
    for h in range(N_SB):
        s[0, col(SB_Q + h)] = LOG2E * HEAD_DIM ** -0.5
    for h in range(N_DIFF):
        s[0, col(DIFF_Q + h)] = LOG2E * (HEAD_DIM // 2) ** -0.5
    return s


def _silu(g):
    return g * (1.0 / (1.0 + jnp.exp(-g)))


def _dot_nt(a, b):
    return lax.dot_general(a, b, (((1,), (1,)), ((), ())), preferred_element_type=F32)


def _dot(a, b):
    return jnp.dot(a, b, preferred_element_type=F32)


def _neg_abs(x):
    bits = lax.bitcast_convert_type(x, jnp.uint32) | jnp.uint32(0x80000000)
    return lax.bitcast_convert_type(bits, F32)


def _lanes2(x):
    return jnp.concatenate([x, x], axis=1)


def _params(*sem, flags=None):
    return pltpu.CompilerParams(dimension_semantics=sem, vmem_limit_bytes=V7X_VMEM_LIMIT, flags=flags)


def _proj_kernel(x_ref, w_ref, s_ref, o_ref, wb_ref):
    @pl.when(pl.program_id(1) == 0)
    def _():
        wb_ref[...] = w_ref[...].astype(BF16)

    o_ref[...] = (_dot(x_ref[...], wb_ref[...]) * s_ref[...]).astype(o_ref.dtype)


def _proj(xb, w_all, layer, col_scale, *, tm=512, tn=1024):
    m, k = xb.shape
    n = w_all.shape[2]
    return pl.pallas_call(
        _proj_kernel,
        grid=(n // tn, m // tm),
        in_specs=[pl.BlockSpec((tm, k), lambda j, i: (i, 0)),
                  pl.BlockSpec((None, k, tn), lambda j, i: (layer, 0, j)),
                  pl.BlockSpec((1, tn), lambda j, i: (0, j))],
        out_specs=pl.BlockSpec((tm, tn), lambda j, i: (i, j)),
        out_shape=jax.ShapeDtypeStruct((m, n), BF16),
        scratch_shapes=[pltpu.VMEM((k, tn), BF16)],
        compiler_params=_params("parallel", "arbitrary"),
        name="proj_matmul",
    )(xb, w_all, col_scale)


def _ret_kernel(lg_ref, q_ref, k_ref, v_ref, g_ref, gn_ref, o_ref,
                state_ref, inner_ref, qd_ref, kd_ref, cd_ref):
    hg = pl.program_id(1)
    nh = RET_HEADS_PER_STEP
    seq = q_ref.shape[0]
    c = CHUNK
    row = lax.broadcasted_iota(jnp.int32, (c, c), 0)
    col = lax.broadcasted_iota(jnp.int32, (c, c), 1)
    rel = (row - col).astype(F32)
    causal = row >= col
    rowf = row.astype(F32)
    for i in range(nh):
        lg = lg_ref[hg * nh + i]
        inner_ref[i] = jnp.where(causal, jnp.exp(lg * jnp.where(causal, rel, 0.0)), 0.0)
        qd_ref[i] = jnp.exp(lg * (rowf + 1.0))
        kd_ref[i] = jnp.exp(lg * (c - 1.0 - col.astype(F32)))
        cd_ref[i] = jnp.exp(lg * (c + 0.0 * rowf))
    state_ref[...] = jnp.zeros_like(state_ref)

    def body(ci, carry):
        r0 = pl.multiple_of(ci * c, c)
        for i in range(nh):
            lanes = slice(i * HEAD_DIM, (i + 1) * HEAD_DIM)
            q = q_ref[pl.ds(r0, c), lanes]
            k = k_ref[pl.ds(r0, c), lanes]
            v = v_ref[pl.ds(r0, c), lanes]
            g = g_ref[pl.ds(r0, c), lanes].astype(F32)
            state = state_ref[i]
            kt = k.astype(F32).T
            qk_qs = _dot(q, jnp.concatenate([kt.astype(BF16), state.astype(BF16)], axis=1))
            scores = (qk_qs[:, :c] * inner_ref[i]).astype(BF16)
            kd_t = (kt * kd_ref[i]).astype(BF16)
            sv_kv = _dot(jnp.concatenate([scores, kd_t], axis=0), v)
            o = sv_kv[:c] + qk_qs[:, c:] * qd_ref[i]
            state_ref[i] = state * cd_ref[i] + sv_kv[c:]
            mu = jnp.mean(o, axis=-1, keepdims=True)
            oc = o - mu
            var = jnp.mean(oc * oc, axis=-1, keepdims=True)
            y = oc * lax.rsqrt(var + LN_EPS) * gn_ref[:, lanes] * _silu(g)
            o_ref[pl.ds(r0, c), lanes] = y.astype(o_ref.dtype)
        return carry

    lax.fori_loop(0, seq // c, body, 0, unroll=4)


def _retention(proj, gn_g, log_gamma, batch, seq):
    nh = RET_HEADS_PER_STEP
    w = nh * HEAD_DIM
    blk = lambda base: pl.BlockSpec((seq, w), lambda b, h: (b, base // nh + h))
    tab = pltpu.VMEM((nh, CHUNK, CHUNK), F32)
    return pl.pallas_call(
        _ret_kernel,
        grid=(batch, N_RET // nh),
        in_specs=[pl.BlockSpec(memory_space=pltpu.SMEM),
                  blk(RET_Q), blk(RET_K), blk(RET_V), blk(RET_G),
                  pl.BlockSpec((1, w), lambda b, h: (0, h))],
        out_specs=pl.BlockSpec((seq, w), lambda b, h: (b, h)),
        out_shape=jax.ShapeDtypeStruct((batch * seq, RET_W), BF16),
        scratch_shapes=[pltpu.VMEM((nh, HEAD_DIM, HEAD_DIM), F32), tab, tab, tab, tab],
        compiler_params=_params("parallel", "parallel"),
        name="retention",
    )(log_gamma, proj, proj, proj, proj, gn_g)


def _sb_parts(q_ref, k_ref, v_ref, g_ref, o_ref, acc_ref, car_ref):
    tq, tk = ATT_TQ, ATT_TK
    sub = tq // tk
    row = lax.broadcasted_iota(jnp.int32, (tk, tk), 0)
    col = lax.broadcasted_iota(jnp.int32, (tk, tk), 1)
    upper = jnp.where(row >= col, 1.0, 0.0).astype(BF16)
    row_in_blk = lax.broadcasted_iota(jnp.int32, (tq, tk), 0) & (tk - 1)
    strict = lax.broadcasted_iota(jnp.int32, (tq, tk), 1) < row_in_blk

    def stage1(z, mask):
        rows = z.shape[0]
        e = jnp.exp2(_neg_abs(z))
        nl = jnp.maximum(z, 0.0) + jnp.log2(1.0 + e)
        if mask is not None:
            nl = jnp.where(mask, nl, 0.0)
        rsum = jnp.broadcast_to(jnp.sum(nl, axis=-1, keepdims=True), (rows, HEAD_DIM))
        return nl.astype(BF16), rsum

    def stage2(z, hl, rsum, r0, mask, v_blocks):
        rows = z.shape[0]
        rs = pl.ds(r0, rows)
        car = car_ref[rs, :]
        a = jnp.exp2(z - (_dot(hl, upper) + _lanes2(car)))
        if mask is not None:
            a = jnp.where(mask, a, 0.0)
        a = a.astype(BF16)
        if len(v_blocks) == 1:
            upd = _dot(a, v_blocks[0])
        else:
            upd = jnp.concatenate(
                [_dot(a[i * tk:(i + 1) * tk], v) for i, v in enumerate(v_blocks)], axis=0)
        acc_ref[rs, :] += upd
        car_ref[rs, :] = car + rsum

    def init(q0):
        acc_ref[...] = jnp.zeros_like(acc_ref)
        car_ref[...] = jnp.zeros_like(car_ref)

    def diag(q0):
        def diag1(d):
            n = sub - d
            q = q_ref[pl.ds(q0 + d * tk, n * tk), :]
            z = jnp.concatenate(
                [_dot_nt(q[i * tk:(i + 1) * tk], k_ref[pl.ds(q0 + i * tk, tk), :]) for i in range(n)],
                axis=0)
            mask = strict if d == 0 else None
            return (z,) + stage1(z, mask)

        def diag2(d, staged):
            n = sub - d
            z, hl, rsum = staged
            mask = strict if d == 0 else None
            stage2(z, hl, rsum, d * tk, mask, [v_ref[pl.ds(q0 + i * tk, tk), :] for i in range(n)])

        staged = diag1(0)
        for d in range(sub):
            nxt = diag1(d + 1) if d + 1 < sub else None
            diag2(d, staged)
            staged = nxt

    def ktile(q0, t):
        for j in range(sub):
            k0 = pl.multiple_of(((t + 1) * sub - 1 - j) * tk, tk)
            z = _dot_nt(q_ref[pl.ds(q0, tq), :], k_ref[pl.ds(k0, tk), :])
            hl, rsum = stage1(z, None)
            stage2(z, hl, rsum, 0, None, [v_ref[pl.ds(k0, tk), :]])

    def final(q0):
        g = g_ref[pl.ds(q0, tq), :].astype(F32)
        o_ref[pl.ds(q0, tq), :] = (acc_ref[...] * _silu(g)).astype(o_ref.dtype)

    return init, diag, ktile, final


def _diff_parts(slope_ref, lamv_ref, q_ref, k_ref, v_ref, g_ref, gain_ref, o_ref,
                q1_ref, q2_ref, accl1_ref, accl2_ref, m1_ref, m2_ref, *, out_scale, lambda_init):
    h = pl.program_id(1)
    sl2 = slope_ref[h] * LOG2E
    tq, tk = ATT_TQ, ATT_TK
    sub = tq // tk
    half = HEAD_DIM // 2
    causal = (lax.broadcasted_iota(jnp.int32, (tk, tk), 1)
              <= lax.broadcasted_iota(jnp.int32, (tk, tk), 0))
    ones = jnp.ones((tk, HEAD_DIM), BF16)
    frame_shift = sl2 * tk
    lane_q = lax.broadcasted_iota(jnp.int32, (tq, HEAD_DIM), 1)
    lane_k = lax.broadcasted_iota(jnp.int32, (tk, HEAD_DIM), 1)
    s_full = jnp.full((tq, HEAD_DIM), sl2, F32)
    s_hi = s_full.astype(BF16).astype(F32)
    s_mid = (s_full - s_hi).astype(BF16).astype(F32)
    s_lo = (s_full - s_hi - s_mid).astype(BF16).astype(F32)

    def slope_feats(base):
        f = jnp.where(lane_q == base, s_hi,
                      jnp.where(lane_q == base + 1, s_mid, jnp.where(lane_q == base + 2, s_lo, 0.0)))
        return f.astype(BF16)

    key_pos = lax.broadcasted_iota(jnp.int32, (tk, HEAD_DIM), 0).astype(F32)

    def pos_feats(base):
        f = jnp.where(lane_k >= base, jnp.where(lane_k < base + 3, key_pos, 0.0), 0.0)
        return f.astype(BF16)

    def keep(lane, first):
        return jnp.where((lane < half) if first else (lane >= half), 1.0, 0.0).astype(BF16)

    q_feats = (slope_feats(half), slope_feats(0))
    k_feats = (pos_feats(half), pos_feats(0))
    q_keep = (keep(lane_q, True), keep(lane_q, False))
    k_keep = (keep(lane_k, True), keep(lane_k, False))

    lv = lamv_ref[...]
    lam = (jnp.exp(jnp.sum(lv[0:1] * lv[1:2], axis=-1, keepdims=True))
           - jnp.exp(jnp.sum(lv[2:3] * lv[3:4], axis=-1, keepdims=True)) + lambda_init)

    maps = ((q1_ref, m1_ref, accl1_ref), (q2_ref, m2_ref, accl2_ref))

    def block(r0, rows, k_starts, mask):
        rs = pl.ds(r0, rows)
        vs = [jnp.concatenate([v_ref[pl.ds(k0, tk), :], ones], axis=1) for k0 in k_starts]
        for mi, (qm_ref, m_ref, accl_ref) in enumerate(maps):
            parts = []
            for i, k0 in enumerate(k_starts):
                k = k_ref[pl.ds(k0, tk), :] * k_keep[mi] + k_feats[mi]
                rows_i = rows if len(k_starts) == 1 else tk
                parts.append(_dot_nt(qm_ref[pl.ds(r0 + i * tk, rows_i), :], k))
            y = parts[0] if len(parts) == 1 else jnp.concatenate(parts, axis=0)
            if mask is not None:
                y = jnp.where(mask, y, NEG)
            m_blk = jnp.broadcast_to(jnp.max(y, axis=-1, keepdims=True), (rows, HEAD_DIM))
            m_old = m_ref[rs, :] - frame_shift
            m_new = jnp.maximum(m_old, m_blk)
            p = jnp.exp2(y - _lanes2(m_new)).astype(BF16)
            alpha = jnp.exp2(m_old - m_new)
            if len(vs) == 1:
                upd = _dot(p, vs[0])
            else:
                upd = jnp.concatenate(
                    [_dot(p[i * tk:(i + 1) * tk], v) for i, v in enumerate(vs)], axis=0)
            accl_ref[rs, :] = _lanes2(alpha) * accl_ref[rs, :] + upd
            m_ref[rs, :] = m_new

    def init(q0):
        q = q_ref[pl.ds(q0, tq), :]
        q1_ref[...] = q * q_keep[0] + q_feats[0]
        q2_ref[...] = q * q_keep[1] + q_feats[1]
        for _, m_ref, accl_ref in maps:
            m_ref[...] = jnp.full_like(m_ref, NEG)
            accl_ref[...] = jnp.zeros_like(accl_ref)

    def ktile(q0, t):
        for j in range(sub):
            block(0, tq, [pl.multiple_of((sub * t + j) * tk, tk)], None)

    def diag(q0):
        for jd in range(sub):
            k0 = q0 + jd * tk
            block(jd * tk, tk, [k0], causal)
            if jd + 1 < sub:
                block((jd + 1) * tk, tq - (jd + 1) * tk, [k0], None)

    def final(q0):
        o1 = accl1_ref[:, :HEAD_DIM] * (1.0 / accl1_ref[:, HEAD_DIM:])
        o2 = accl2_ref[:, :HEAD_DIM] * (1.0 / accl2_ref[:, HEAD_DIM:])
        o = o1 - lam * o2
        o = o * lax.rsqrt(jnp.mean(o * o, axis=-1, keepdims=True) + LN_EPS)
        g = g_ref[pl.ds(q0, tq), :].astype(F32)
        o_ref[pl.ds(q0, tq), :] = (o * (gain_ref[...] * out_scale) * _silu(g)).astype(o_ref.dtype)

    return init, diag, ktile, final


def _attn_kernel(slope_ref, lamv_ref, sq_ref, sk_ref, sv_ref, sg_ref, dq_ref, dk_ref, dv_ref, dg_ref,
                 gain_ref, so_ref, do_ref, acc_ref, car_ref,
                 q1_ref, q2_ref, accl1_ref, accl2_ref, m1_ref, m2_ref, *, out_scale, lambda_init):
    seq = sq_ref.shape[0]
    tq = ATT_TQ
    sb_init, sb_diag, sb_ktile, sb_final = _sb_parts(sq_ref, sk_ref, sv_ref, sg_ref, so_ref, acc_ref, car_ref)
    df_init, df_diag, df_ktile, df_final = _diff_parts(
        slope_ref, lamv_ref, dq_ref, dk_ref, dv_ref, dg_ref, gain_ref, do_ref,
        q1_ref, q2_ref, accl1_ref, accl2_ref, m1_ref, m2_ref,
        out_scale=out_scale, lambda_init=lambda_init)
    for qi in range(seq // tq):
        q0 = qi * tq
        sb_init(q0)
        df_init(q0)
        sb_diag(q0)

        def trip(i, c, q0=q0, qi=qi):
            sb_ktile(q0, qi - 1 - i)
            df_ktile(q0, i)
            return c

        lax.fori_loop(0, qi, trip, 0)
        df_diag(q0)
        sb_final(q0)
        df_final(q0)


def _attention(proj, gain, lamv, slopes, batch, seq, *, lambda_init):
    assert N_SB == N_DIFF
    blk = lambda base: pl.BlockSpec((seq, HEAD_DIM), lambda b, h: (b, base + h))
    out_blk = pl.BlockSpec((seq, HEAD_DIM), lambda b, h: (b, h))
    rows128 = pltpu.VMEM((ATT_TQ, HEAD_DIM), F32)
    qm = pltpu.VMEM((ATT_TQ, HEAD_DIM), BF16)
    accl = pltpu.VMEM((ATT_TQ, 2 * HEAD_DIM), F32)
    return pl.pallas_call(
        functools.partial(_attn_kernel, out_scale=1.0 - lambda_init, lambda_init=lambda_init),
        grid=(batch, N_SB),
        in_specs=[pl.BlockSpec(memory_space=pltpu.SMEM),
                  pl.BlockSpec((4, HEAD_DIM // 2), lambda b, h: (0, 0)),
                  blk(SB_Q), blk(SB_K), blk(SB_V), blk(SB_G),
                  blk(DIFF_Q), blk(DIFF_K), blk(DIFF_V), blk(DIFF_G),
                  pl.BlockSpec((1, HEAD_DIM), lambda b, h: (0, h))],
        out_specs=[out_blk, out_blk],
        out_shape=[jax.ShapeDtypeStruct((batch * seq, SB_W), BF16),
                   jax.ShapeDtypeStruct((batch * seq, DIFF_W), BF16)],
        scratch_shapes=[rows128, rows128, qm, qm, accl, accl, rows128, rows128],
        compiler_params=_params("parallel", "parallel"),
        name="sb_diff_attention",
    )(slopes, lamv, proj, proj, proj, proj, proj, proj, proj, proj, gain)


def _outproj_kernel(ro_ref, so_ref, do_ref, w_ref, x_ref, o_ref, *, alpha):
    y = _dot(ro_ref[...], w_ref[0:RET_W, :])
    y = y + _dot(so_ref[...], w_ref[RET_W:RET_W + SB_W, :])
    y = y + _dot(do_ref[...], w_ref[RET_W + SB_W:, :])
    o_ref[...] = alpha * x_ref[...] + y


def _outproj(ro, so, do, w_all, layer, x, *, alpha, tm=1024, tn=1024):
    m = ro.shape[0]
    k, n = w_all.shape[1], w_all.shape[2]
    row_blk = lambda w: pl.BlockSpec((tm, w), lambda i, j: (i, 0))
    return pl.pallas_call(
        functools.partial(_outproj_kernel, alpha=alpha),
        grid=(m // tm, n // tn),
        in_specs=[row_blk(RET_W), row_blk(SB_W), row_blk(DIFF_W),
                  pl.BlockSpec((None, k, tn), lambda i, j: (layer, 0, j)),
                  pl.BlockSpec((tm, tn), lambda i, j: (i, j))],
        out_specs=pl.BlockSpec((tm, tn), lambda i, j: (i, j)),
        out_shape=jax.ShapeDtypeStruct((m, n), F32),
        compiler_params=_params("parallel", "parallel"),
        name="out_proj_residual",
    )(ro, so, do, w_all, x)


def _ln_kernel(z_ref, g_ref, b_ref, o_ref, ob_ref):
    z = z_ref[...]
    mu = jnp.mean(z, axis=-1, keepdims=True)
    zc = z - mu
    var = jnp.mean(zc * zc, axis=-1, keepdims=True)
    y = zc * lax.rsqrt(var + LN_EPS) * g_ref[...] + b_ref[...]
    o_ref[...] = y
    ob_ref[...] = y.astype(ob_ref.dtype)


def _layer_norm(z, g, b, *, tr=256):
    m, d = z.shape
    return pl.pallas_call(
        _ln_kernel,
        grid=(m // tr,),
        in_specs=[pl.BlockSpec((tr, d), lambda i: (i, 0)),
                  pl.BlockSpec((1, d), lambda i: (0, 0)),
                  pl.BlockSpec((1, d), lambda i: (0, 0))],
        out_specs=[pl.BlockSpec((tr, d), lambda i: (i, 0)),
                   pl.BlockSpec((tr, d), lambda i: (i, 0))],
        out_shape=[jax.ShapeDtypeStruct((m, d), F32), jax.ShapeDtypeStruct((m, d), BF16)],
        compiler_params=_params("parallel"),
        name="layer_norm",
    )(z, g, b)


def kernel(x, w_in, w_out, ln_g, ln_b, ret_gn_g, diff_ln_g, lam_q1, lam_k1, lam_q2, lam_k2):
    batch, seq, d_model = x.shape
    depth = w_in.shape[0]
    alpha = (2.0 * depth) ** 0.25
    log_gamma = jnp.asarray(_retention_log_decay(N_RET))
    slopes = jnp.asarray(_alibi_slopes(N_DIFF))
    col_scale = jnp.asarray(_proj_col_scale())
    w_out_b = w_out.astype(BF16)

    xf = x.reshape(batch * seq, d_model)
    xb = xf.astype(BF16)
    for layer in range(depth):
        lambda_init = 0.8 - 0.6 * math.exp(-0.3 * layer)
        proj = _proj(xb, w_in, layer, col_scale)
        ro = _retention(proj, ret_gn_g[layer][None, :], log_gamma, batch, seq)
        lamv = jnp.stack([lam_q1[layer], lam_k1[layer], lam_q2[layer], lam_k2[layer]]).astype(F32)
        so, do = _attention(proj, diff_ln_g[layer][None, :], lamv, slopes, batch, seq,
                            lambda_init=lambda_init)
        z = _outproj(ro, so, do, w_out_b, layer, xf, alpha=alpha)
        xf, xb = _layer_norm(z, ln_g[layer][None, :], ln_b[layer][None, :])
    return xf.reshape(batch, seq, d_model)
```

```python
import functools
import math

import jax
import jax.numpy as jnp
import numpy as np
from jax import lax
from jax.experimental import pallas as pl
from jax.experimental.pallas import tpu as pltpu

HEAD_DIM = 128
N_RET = 12
N_SB = 10
N_DIFF = 10
RET_W = N_RET * HEAD_DIM
SB_W = N_SB * HEAD_DIM
DIFF_W = N_DIFF * HEAD_DIM
PROJ_W = 4 * (RET_W + SB_W + DIFF_W)
CHUNK = 128
LN_EPS = 1e-5
LOG2E = math.log2(math.e)

RET_Q, RET_K, RET_V, RET_G = 0, N_RET, 2 * N_RET, 3 * N_RET
SB_BASE = 4 * N_RET
SB_Q, SB_K, SB_V, SB_G = SB_BASE, SB_BASE + N_SB, SB_BASE + 2 * N_SB, SB_BASE + 3 * N_SB
DIFF_BASE = SB_BASE + 4 * N_SB
DIFF_Q, DIFF_K, DIFF_V, DIFF_G = (DIFF_BASE, DIFF_BASE + N_DIFF, DIFF_BASE + 2 * N_DIFF,
                                  DIFF_BASE + 3 * N_DIFF)

RET_HEADS_PER_STEP = 4
ATT_TQ = 1024
ATT_TK = 256
NEG = -1e30

V7X_VMEM_LIMIT = 56 * 1024 * 1024

F32 = jnp.float32
BF16 = jnp.bfloat16


def _alibi_slopes(n):
    def pow2(m):
        start = 2.0 ** (-8.0 / m)
        return [start ** (i + 1) for i in range(m)]
    if math.log2(n).is_integer():
        s = pow2(n)
    else:
        c = 2 ** math.floor(math.log2(n))
        s = pow2(c) + pow2(2 * c)[0::2][: n - c]
    return np.asarray(s, dtype=np.float32)


def _retention_log_decay(n):
    g = 1.0 - 2.0 ** (-5.0 - np.arange(n))
    return np.log(g).astype(np.float32)


def _proj_col_scale():
    s = np.ones((1, PROJ_W), np.float32)
    col = lambda blk: slice(blk * HEAD_DIM, (blk + 1) * HEAD_DIM)
    for h in range(N_RET):
        s[0, col(RET_K + h)] = HEAD_DIM ** -0.5
    for h in range(N_SB):
        s[0, col(SB_Q + h)] = LOG2E * HEAD_DIM ** -0.5
    for h in range(N_DIFF):
        s[0, col(DIFF_Q + h)] = LOG2E * (HEAD_DIM // 2) ** -0.5
    return s


def _silu(g):
    return g * (1.0 / (1.0 + jnp.exp(-g)))


def _dot_nt(a, b):
    return lax.dot_general(a, b, (((1,), (1,)), ((), ())), preferred_element_type=F32)


def _dot(a, b):
    return jnp.dot(a, b, preferred_element_type=F32)


def _neg_abs(x):
    bits = lax.bitcast_convert_type(x, jnp.uint32) | jnp.uint32(0x80000000)
    return lax.bitcast_convert_type(bits, F32)


def _lanes2(x):
    return jnp.concatenate([x, x], axis=1)


def _params(*sem, flags=None):
    return pltpu.CompilerParams(dimension_semantics=sem, vmem_limit_bytes=V7X_VMEM_LIMIT, flags=flags)


def _proj_kernel(x_ref, w_ref, s_ref, o_ref):
    w = w_ref[...].astype(BF16)
    o_ref[...] = (_dot(x_ref[...], w) * s_ref[...]).astype(o_ref.dtype)


def _proj(xb, w_all, layer, col_scale, *, tm=1024, tn=512):
    m, k = xb.shape
    n = w_all.shape[2]
    return pl.pallas_call(
        _proj_kernel,
        grid=(m // tm, n // tn),
        in_specs=[pl.BlockSpec((tm, k), lambda i, j: (i, 0)),
                  pl.BlockSpec((None, k, tn), lambda i, j: (layer, 0, j)),
                  pl.BlockSpec((1, tn), lambda i, j: (0, j))],
        out_specs=pl.BlockSpec((tm, tn), lambda i, j: (i, j)),
        out_shape=jax.ShapeDtypeStruct((m, n), BF16),
        compiler_params=_params("parallel", "parallel"),
        name="proj_matmul",
    )(xb, w_all, col_scale)


def _ret_kernel(lg_ref, q_ref, k_ref, v_ref, g_ref, gn_ref, o_ref,
                state_ref, inner_ref, qd_ref, kd_ref, cd_ref):
    hg = pl.program_id(1)
    nh = RET_HEADS_PER_STEP
    seq = q_ref.shape[0]
    c = CHUNK
    row = lax.broadcasted_iota(jnp.int32, (c, c), 0)
    col = lax.broadcasted_iota(jnp.int32, (c, c), 1)
    rel = (row - col).astype(F32)
    causal = row >= col
    rowf = row.astype(F32)
    for i in range(nh):
        lg = lg_ref[hg * nh + i]
        inner_ref[i] = jnp.where(causal, jnp.exp(lg * jnp.where(causal, rel, 0.0)), 0.0)
        qd_ref[i] = jnp.exp(lg * (rowf + 1.0))
        kd_ref[i] = jnp.exp(lg * (c - 1.0 - col.astype(F32)))
        cd_ref[i] = jnp.exp(lg * (c + 0.0 * rowf))
    state_ref[...] = jnp.zeros_like(state_ref)

    def body(ci, carry):
        r0 = pl.multiple_of(ci * c, c)
        for i in range(nh):
            lanes = slice(i * HEAD_DIM, (i + 1) * HEAD_DIM)
            q = q_ref[pl.ds(r0, c), lanes]
            k = k_ref[pl.ds(r0, c), lanes]
            v = v_ref[pl.ds(r0, c), lanes]
            g = g_ref[pl.ds(r0, c), lanes].astype(F32)
            state = state_ref[i]
            kt = k.astype(F32).T
            qk_qs = _dot(q, jnp.concatenate([kt.astype(BF16), state.astype(BF16)], axis=1))
            scores = (qk_qs[:, :c] * inner_ref[i]).astype(BF16)
            kd_t = (kt * kd_ref[i]).astype(BF16)
            sv_kv = _dot(jnp.concatenate([scores, kd_t], axis=0), v)
            o = sv_kv[:c] + qk_qs[:, c:] * qd_ref[i]
            state_ref[i] = state * cd_ref[i] + sv_kv[c:]
            mu = jnp.mean(o, axis=-1, keepdims=True)
            oc = o - mu
            var = jnp.mean(oc * oc, axis=-1, keepdims=True)
            y = oc * lax.rsqrt(var + LN_EPS) * gn_ref[:, lanes] * _silu(g)
            o_ref[pl.ds(r0, c), lanes] = y.astype(o_ref.dtype)
        return carry

    lax.fori_loop(0, seq // c, body, 0, unroll=4)


def _retention(proj, gn_g, log_gamma, batch, seq):
    nh = RET_HEADS_PER_STEP
    w = nh * HEAD_DIM
    blk = lambda base: pl.BlockSpec((seq, w), lambda b, h: (b, base // nh + h))
    tab = pltpu.VMEM((nh, CHUNK, CHUNK), F32)
    return pl.pallas_call(
        _ret_kernel,
        grid=(batch, N_RET // nh),
        in_specs=[pl.BlockSpec(memory_space=pltpu.SMEM),
                  blk(RET_Q), blk(RET_K), blk(RET_V), blk(RET_G),
                  pl.BlockSpec((1, w), lambda b, h: (0, h))],
        out_specs=pl.BlockSpec((seq, w), lambda b, h: (b, h)),
        out_shape=jax.ShapeDtypeStruct((batch * seq, RET_W), BF16),
        scratch_shapes=[pltpu.VMEM((nh, HEAD_DIM, HEAD_DIM), F32), tab, tab, tab, tab],
        compiler_params=_params("parallel", "parallel"),
        name="retention",
    )(log_gamma, proj, proj, proj, proj, gn_g)


def _sb_parts(q_ref, k_ref, v_ref, g_ref, o_ref, acc_ref, car_ref):
    tq, tk = ATT_TQ, ATT_TK
    sub = tq // tk
    row = lax.broadcasted_iota(jnp.int32, (tk, tk), 0)
    col = lax.broadcasted_iota(jnp.int32, (tk, tk), 1)
    upper = jnp.where(row >= col, 1.0, 0.0).astype(BF16)
    row_in_blk = lax.broadcasted_iota(jnp.int32, (tq, tk), 0) & (tk - 1)
    strict = lax.broadcasted_iota(jnp.int32, (tq, tk), 1) < row_in_blk

    def stage1(z, mask):
        rows = z.shape[0]
        e = jnp.exp2(_neg_abs(z))
        nl = jnp.maximum(z, 0.0) + jnp.log2(1.0 + e)
        if mask is not None:
            nl = jnp.where(mask, nl, 0.0)
        rsum = jnp.broadcast_to(jnp.sum(nl, axis=-1, keepdims=True), (rows, HEAD_DIM))
        return nl.astype(BF16), rsum

    def stage2(z, hl, rsum, r0, mask, v_blocks):
        rows = z.shape[0]
        rs = pl.ds(r0, rows)
        car = car_ref[rs, :]
        a = jnp.exp2(z - (_dot(hl, upper) + _lanes2(car)))
        if mask is not None:
            a = jnp.where(mask, a, 0.0)
        a = a.astype(BF16)
        if len(v_blocks) == 1:
            upd = _dot(a, v_blocks[0])
        else:
            upd = jnp.concatenate(
                [_dot(a[i * tk:(i + 1) * tk], v) for i, v in enumerate(v_blocks)], axis=0)
        acc_ref[rs, :] += upd
        car_ref[rs, :] = car + rsum

    def init(q0):
        acc_ref[...] = jnp.zeros_like(acc_ref)
        car_ref[...] = jnp.zeros_like(car_ref)

    def diag(q0):
        def diag1(d):
            n = sub - d
            q = q_ref[pl.ds(q0 + d * tk, n * tk), :]
            z = jnp.concatenate(
                [_dot_nt(q[i * tk:(i + 1) * tk], k_ref[pl.ds(q0 + i * tk, tk), :]) for i in range(n)],
                axis=0)
            mask = strict if d == 0 else None
            return (z,) + stage1(z, mask)

        def diag2(d, staged):
            n = sub - d
            z, hl, rsum = staged
            mask = strict if d == 0 else None
            stage2(z, hl, rsum, d * tk, mask, [v_ref[pl.ds(q0 + i * tk, tk), :] for i in range(n)])

        staged = diag1(0)
        for d in range(sub):
            nxt = diag1(d + 1) if d + 1 < sub else None
            diag2(d, staged)
            staged = nxt

    def ktile(q0, t):
        for j in range(sub):
            k0 = pl.multiple_of(((t + 1) * sub - 1 - j) * tk, tk)
            z = _dot_nt(q_ref[pl.ds(q0, tq), :], k_ref[pl.ds(k0, tk), :])
            hl, rsum = stage1(z, None)
            stage2(z, hl, rsum, 0, None, [v_ref[pl.ds(k0, tk), :]])

    def final(q0):
        g = g_ref[pl.ds(q0, tq), :].astype(F32)
        o_ref[pl.ds(q0, tq), :] = (acc_ref[...] * _silu(g)).astype(o_ref.dtype)

    return init, diag, ktile, final


def _diff_parts(slope_ref, lamv_ref, q_ref, k_ref, v_ref, g_ref, gain_ref, o_ref,
                q1_ref, q2_ref, accl1_ref, accl2_ref, m1_ref, m2_ref, *, out_scale, lambda_init):
    h = pl.program_id(1)
    sl2 = slope_ref[h] * LOG2E
    tq, tk = ATT_TQ, ATT_TK
    sub = tq // tk
    half = HEAD_DIM // 2
    causal = (lax.broadcasted_iota(jnp.int32, (tk, tk), 1)
              <= lax.broadcasted_iota(jnp.int32, (tk, tk), 0))
    ones = jnp.ones((tk, HEAD_DIM), BF16)
    frame_shift = sl2 * tk
    lane_q = lax.broadcasted_iota(jnp.int32, (tq, HEAD_DIM), 1)
    lane_k = lax.broadcasted_iota(jnp.int32, (tk, HEAD_DIM), 1)
    s_full = jnp.full((tq, HEAD_DIM), sl2, F32)
    s_hi = s_full.astype(BF16).astype(F32)
    s_mid = (s_full - s_hi).astype(BF16).astype(F32)
    s_lo = (s_full - s_hi - s_mid).astype(BF16).astype(F32)

    def slope_feats(base):
        f = jnp.where(lane_q == base, s_hi,
                      jnp.where(lane_q == base + 1, s_mid, jnp.where(lane_q == base + 2, s_lo, 0.0)))
        return f.astype(BF16)

    key_pos = lax.broadcasted_iota(jnp.int32, (tk, HEAD_DIM), 0).astype(F32)

    def pos_feats(base):
        f = jnp.where(lane_k >= base, jnp.where(lane_k < base + 3, key_pos, 0.0), 0.0)
        return f.astype(BF16)

    def keep(lane, first):
        return jnp.where((lane < half) if first else (lane >= half), 1.0, 0.0).astype(BF16)

    q_feats = (slope_feats(half), slope_feats(0))
    k_feats = (pos_feats(half), pos_feats(0))
    q_keep = (keep(lane_q, True), keep(lane_q, False))
    k_keep = (keep(lane_k, True), keep(lane_k, False))

    lv = lamv_ref[...]
    lam = (jnp.exp(jnp.sum(lv[0:1] * lv[1:2], axis=-1, keepdims=True))
           - jnp.exp(jnp.sum(lv[2:3] * lv[3:4], axis=-1, keepdims=True)) + lambda_init)

    maps = ((q1_ref, m1_ref, accl1_ref), (q2_ref, m2_ref, accl2_ref))

    def block(r0, rows, k_starts, mask):
        rs = pl.ds(r0, rows)
        vs = [jnp.concatenate([v_ref[pl.ds(k0, tk), :], ones], axis=1) for k0 in k_starts]
        for mi, (qm_ref, m_ref, accl_ref) in enumerate(maps):
            parts = []
            for i, k0 in enumerate(k_starts):
                k = k_ref[pl.ds(k0, tk), :] * k_keep[mi] + k_feats[mi]
                rows_i = rows if len(k_starts) == 1 else tk
                parts.append(_dot_nt(qm_ref[pl.ds(r0 + i * tk, rows_i), :], k))
            y = parts[0] if len(parts) == 1 else jnp.concatenate(parts, axis=0)
            if mask is not None:
                y = jnp.where(mask, y, NEG)
            m_blk = jnp.broadcast_to(jnp.max(y, axis=-1, keepdims=True), (rows, HEAD_DIM))
            m_old = m_ref[rs, :] - frame_shift
            m_new = jnp.maximum(m_old, m_blk)
            p = jnp.exp2(y - _lanes2(m_new)).astype(BF16)
            alpha = jnp.exp2(m_old - m_new)
            if len(vs) == 1:
                upd = _dot(p, vs[0])
            else:
                upd = jnp.concatenate(
                    [_dot(p[i * tk:(i + 1) * tk], v) for i, v in enumerate(vs)], axis=0)
            accl_ref[rs, :] = _lanes2(alpha) * accl_ref[rs, :] + upd
            m_ref[rs, :] = m_new

    def init(q0):
        q = q_ref[pl.ds(q0, tq), :]
        q1_ref[...] = q * q_keep[0] + q_feats[0]
        q2_ref[...] = q * q_keep[1] + q_feats[1]
        for _, m_ref, accl_ref in maps:
            m_ref[...] = jnp.full_like(m_ref, NEG)
            accl_ref[...] = jnp.zeros_like(accl_ref)

    def ktile(q0, t):
        for j in range(sub):
            block(0, tq, [pl.multiple_of((sub * t + j) * tk, tk)], None)

    def diag(q0):
        for jd in range(sub):
            k0 = q0 + jd * tk
            block(jd * tk, tk, [k0], causal)
            if jd + 1 < sub:
                block((jd + 1) * tk, tq - (jd + 1) * tk, [k0], None)

    def final(q0):
        o1 = accl1_ref[:, :HEAD_DIM] * (1.0 / accl1_ref[:, HEAD_DIM:])
        o2 = accl2_ref[:, :HEAD_DIM] * (1.0 / accl2_ref[:, HEAD_DIM:])
        o = o1 - lam * o2
        o = o * lax.rsqrt(jnp.mean(o * o, axis=-1, keepdims=True) + LN_EPS)
        g = g_ref[pl.ds(q0, tq), :].astype(F32)
        o_ref[pl.ds(q0, tq), :] = (o * (gain_ref[...] * out_scale) * _silu(g)).astype(o_ref.dtype)

    return init, diag, ktile, final


def _attn_kernel(slope_ref, lamv_ref, sq_ref, sk_ref, sv_ref, sg_ref, dq_ref, dk_ref, dv_ref, dg_ref,
                 gain_ref, so_ref, do_ref, acc_ref, car_ref,
                 q1_ref, q2_ref, accl1_ref, accl2_ref, m1_ref, m2_ref, *, out_scale, lambda_init):
    seq = sq_ref.shape[0]
    tq = ATT_TQ
    sb_init, sb_diag, sb_ktile, sb_final = _sb_parts(sq_ref, sk_ref, sv_ref, sg_ref, so_ref, acc_ref, car_ref)
    df_init, df_diag, df_ktile, df_final = _diff_parts(
        slope_ref, lamv_ref, dq_ref, dk_ref, dv_ref, dg_ref, gain_ref, do_ref,
        q1_ref, q2_ref, accl1_ref, accl2_ref, m1_ref, m2_ref,
        out_scale=out_scale, lambda_init=lambda_init)
    for qi in range(seq // tq):
        q0 = qi * tq
        sb_init(q0)
        df_init(q0)
        sb_diag(q0)

        def trip(i, c, q0=q0, qi=qi):
            sb_ktile(q0, qi - 1 - i)
            df_ktile(q0, i)
            return c

        lax.fori_loop(0, qi, trip, 0)
        df_diag(q0)
        sb_final(q0)
        df_final(q0)


def _attention(proj, gain, lamv, slopes, batch, seq, *, lambda_init):
    assert N_SB == N_DIFF
    blk = lambda base: pl.BlockSpec((seq, HEAD_DIM), lambda b, h: (b, base + h))
    out_blk = pl.BlockSpec((seq, HEAD_DIM), lambda b, h: (b, h))
    rows128 = pltpu.VMEM((ATT_TQ, HEAD_DIM), F32)
    qm = pltpu.VMEM((ATT_TQ, HEAD_DIM), BF16)
    accl = pltpu.VMEM((ATT_TQ, 2 * HEAD_DIM), F32)
    return pl.pallas_call(
        functools.partial(_attn_kernel, out_scale=1.0 - lambda_init, lambda_init=lambda_init),
        grid=(batch, N_SB),
        in_specs=[pl.BlockSpec(memory_space=pltpu.SMEM),
                  pl.BlockSpec((4, HEAD_DIM // 2), lambda b, h: (0, 0)),
                  blk(SB_Q), blk(SB_K), blk(SB_V), blk(SB_G),
                  blk(DIFF_Q), blk(DIFF_K), blk(DIFF_V), blk(DIFF_G),
                  pl.BlockSpec((1, HEAD_DIM), lambda b, h: (0, h))],
        out_specs=[out_blk, out_blk],
        out_shape=[jax.ShapeDtypeStruct((batch * seq, SB_W), BF16),
                   jax.ShapeDtypeStruct((batch * seq, DIFF_W), BF16)],
        scratch_shapes=[rows128, rows128, qm, qm, accl, accl, rows128, rows128],
        compiler_params=_params("parallel", "parallel"),
        name="sb_diff_attention",
    )(slopes, lamv, proj, proj, proj, proj, proj, proj, proj, proj, gain)


def _outproj_kernel(ro_ref, so_ref, do_ref, w_ref, z_ref, mu_ref, rs_ref, g_ref, b_ref, o_ref, *, alpha):
    y = _dot(ro_ref[...], w_ref[0:RET_W, :])
    y = y + _dot(so_ref[...], w_ref[RET_W:RET_W + SB_W, :])
    y = y + _dot(do_ref[...], w_ref[RET_W + SB_W:, :])
    reps = z_ref.shape[1] // HEAD_DIM
    mu = jnp.concatenate([mu_ref[...]] * reps, axis=1)
    rs = jnp.concatenate([rs_ref[...]] * reps, axis=1)
    x = (z_ref[...] - mu) * rs * g_ref[...] + b_ref[...]
    o_ref[...] = alpha * x + y


def _outproj(ro, so, do, w_all, layer, z_prev, mu, rs, g, b, *, alpha, tm=1024, tn=1024):
    m = ro.shape[0]
    k, n = w_all.shape[1], w_all.shape[2]
    row_blk = lambda w: pl.BlockSpec((tm, w), lambda i, j: (i, 0))
    col_vec = pl.BlockSpec((1, tn), lambda i, j: (0, j))
    return pl.pallas_call(
        functools.partial(_outproj_kernel, alpha=alpha),
        grid=(m // tm, n // tn),
        in_specs=[row_blk(RET_W), row_blk(SB_W), row_blk(DIFF_W),
                  pl.BlockSpec((None, k, tn), lambda i, j: (layer, 0, j)),
                  pl.BlockSpec((tm, tn), lambda i, j: (i, j)),
                  row_blk(HEAD_DIM), row_blk(HEAD_DIM), col_vec, col_vec],
        out_specs=pl.BlockSpec((tm, tn), lambda i, j: (i, j)),
        out_shape=jax.ShapeDtypeStruct((m, n), F32),
        compiler_params=_params("parallel", "parallel"),
        name="out_proj_residual",
    )(ro, so, do, w_all, z_prev, mu, rs, g, b)


def _ln_rows(z_ref):
    z = z_ref[...]
    mu = jnp.mean(z, axis=-1, keepdims=True)
    zc = z - mu
    var = jnp.mean(zc * zc, axis=-1, keepdims=True)
    return zc, mu, lax.rsqrt(var + LN_EPS)


def _ln_stats_kernel(z_ref, g_ref, b_ref, ob_ref, mu_ref, rs_ref):
    zc, mu, rs = _ln_rows(z_ref)
    ob_ref[...] = (zc * rs * g_ref[...] + b_ref[...]).astype(ob_ref.dtype)
    mu_ref[...] = jnp.broadcast_to(mu, mu_ref.shape)
    rs_ref[...] = jnp.broadcast_to(rs, rs_ref.shape)


def _ln_final_kernel(z_ref, g_ref, b_ref, o_ref):
    zc, _, rs = _ln_rows(z_ref)
    o_ref[...] = zc * rs * g_ref[...] + b_ref[...]


def _layer_norm(z, g, b, *, final, tr=256):
    m, d = z.shape
    rows = lambda w: pl.BlockSpec((tr, w), lambda i: (i, 0))
    vec = pl.BlockSpec((1, d), lambda i: (0, 0))
    if final:
        kern, out_specs = _ln_final_kernel, rows(d)
        out_shape = jax.ShapeDtypeStruct((m, d), F32)
    else:
        kern, out_specs = _ln_stats_kernel, [rows(d), rows(HEAD_DIM), rows(HEAD_DIM)]
        out_shape = [jax.ShapeDtypeStruct((m, d), BF16), jax.ShapeDtypeStruct((m, HEAD_DIM), F32),
                     jax.ShapeDtypeStruct((m, HEAD_DIM), F32)]
    return pl.pallas_call(
        kern,
        grid=(m // tr,),
        in_specs=[rows(d), vec, vec],
        out_specs=out_specs,
        out_shape=out_shape,
        compiler_params=_params("parallel"),
        name="layer_norm",
    )(z, g, b)


def kernel(x, w_in, w_out, ln_g, ln_b, ret_gn_g, diff_ln_g, lam_q1, lam_k1, lam_q2, lam_k2):
    batch, seq, d_model = x.shape
    depth = w_in.shape[0]
    alpha = (2.0 * depth) ** 0.25
    log_gamma = jnp.asarray(_retention_log_decay(N_RET))
    slopes = jnp.asarray(_alibi_slopes(N_DIFF))
    col_scale = jnp.asarray(_proj_col_scale())
    w_out_b = w_out.astype(BF16)

    m = batch * seq
    z = x.reshape(m, d_model)
    xb = z.astype(BF16)
    mu = jnp.zeros((m, HEAD_DIM), F32)
    rs = jnp.ones((m, HEAD_DIM), F32)
    g = jnp.ones((1, d_model), F32)
    b = jnp.zeros((1, d_model), F32)
    for layer in range(depth):
        lambda_init = 0.8 - 0.6 * math.exp(-0.3 * layer)
        proj = _proj(xb, w_in, layer, col_scale)
        ro = _retention(proj, ret_gn_g[layer][None, :], log_gamma, batch, seq)
        lamv = jnp.stack([lam_q1[layer], lam_k1[layer], lam_q2[layer], lam_k2[layer]]).astype(F32)
        so, do = _attention(proj, diff_ln_g[layer][None, :], lamv, slopes, batch, seq,
                            lambda_init=lambda_init)
        z = _outproj(ro, so, do, w_out_b, layer, z, mu, rs, g, b, alpha=alpha)
        g, b = ln_g[layer][None, :], ln_b[layer][None, :]
        if layer + 1 < depth:
            xb, mu, rs = _layer_norm(z, g, b, final=False)
    return _layer_norm(z, g, b, final=True).reshape(batch, seq, d_model)
```

```python
import functools
import math

import jax
import jax.numpy as jnp
import numpy as np
from jax import lax
from jax.experimental import pallas as pl
from jax.experimental.pallas import tpu as pltpu

HEAD_DIM = 128
N_RET = 12
N_SB = 10
N_DIFF = 10
RET_W = N_RET * HEAD_DIM
SB_W = N_SB * HEAD_DIM
DIFF_W = N_DIFF * HEAD_DIM
PROJ_W = 4 * (RET_W + SB_W + DIFF_W)
CHUNK = 128
LN_EPS = 1e-5
LOG2E = math.log2(math.e)

RET_Q, RET_K, RET_V, RET_G = 0, N_RET, 2 * N_RET, 3 * N_RET
SB_BASE = 4 * N_RET
SB_Q, SB_K, SB_V, SB_G = SB_BASE, SB_BASE + N_SB, SB_BASE + 2 * N_SB, SB_BASE + 3 * N_SB
DIFF_BASE = SB_BASE + 4 * N_SB
DIFF_Q, DIFF_K, DIFF_V, DIFF_G = (DIFF_BASE, DIFF_BASE + N_DIFF, DIFF_BASE + 2 * N_DIFF,
                                  DIFF_BASE + 3 * N_DIFF)

RET_HEADS_PER_STEP = 4
ATT_TQ = 1024
ATT_TK = 256
NEG = -1e30

V7X_VMEM_LIMIT = 56 * 1024 * 1024

F32 = jnp.float32
BF16 = jnp.bfloat16


def _alibi_slopes(n):
    def pow2(m):
        start = 2.0 ** (-8.0 / m)
        return [start ** (i + 1) for i in range(m)]
    if math.log2(n).is_integer():
        s = pow2(n)
    else:
        c = 2 ** math.floor(math.log2(n))
        s = pow2(c) + pow2(2 * c)[0::2][: n - c]
    return np.asarray(s, dtype=np.float32)


def _retention_log_decay(n):
    g = 1.0 - 2.0 ** (-5.0 - np.arange(n))
    return np.log(g).astype(np.float32)


def _proj_col_scale():
    s = np.ones((1, PROJ_W), np.float32)
    col = lambda blk: slice(blk * HEAD_DIM, (blk + 1) * HEAD_DIM)
    for h in range(N_RET):
        s[0, col(RET_K + h)] = HEAD_DIM ** -0.5
    for h in range(N_SB):
        s[0, col(SB_Q + h)] = LOG2E * HEAD_DIM ** -0.5
    for h in range(N_DIFF):
        s[0, col(DIFF_Q + h)] = LOG2E * (HEAD_DIM // 2) ** -0.5
    return s


def _silu(g):
    return g * (1.0 / (1.0 + jnp.exp(-g)))


def _dot_nt(a, b):
    return lax.dot_general(a, b, (((1,), (1,)), ((), ())), preferred_element_type=F32)


def _dot(a, b):
    return jnp.dot(a, b, preferred_element_type=F32)


def _neg_abs(x):
    bits = lax.bitcast_convert_type(x, jnp.uint32) | jnp.uint32(0x80000000)
    return lax.bitcast_convert_type(bits, F32)


def _lanes2(x):
    return jnp.concatenate([x, x], axis=1)


def _params(*sem, flags=None):
    return pltpu.CompilerParams(dimension_semantics=sem, vmem_limit_bytes=V7X_VMEM_LIMIT, flags=flags)


def _proj_kernel(x_ref, w_ref, s_ref, o_ref):
    w = w_ref[...].astype(BF16)
    o_ref[...] = (_dot(x_ref[...], w) * s_ref[...]).astype(o_ref.dtype)


def _proj(xb, w_all, layer, col_scale, *, tm=1024, tn=512):
    m, k = xb.shape
    n = w_all.shape[2]
    return pl.pallas_call(
        _proj_kernel,
        grid=(m // tm, n // tn),
        in_specs=[pl.BlockSpec((tm, k), lambda i, j: (i, 0)),
                  pl.BlockSpec((None, k, tn), lambda i, j: (layer, 0, j)),
                  pl.BlockSpec((1, tn), lambda i, j: (0, j))],
        out_specs=pl.BlockSpec((tm, tn), lambda i, j: (i, j)),
        out_shape=jax.ShapeDtypeStruct((m, n), BF16),
        compiler_params=_params("parallel", "parallel"),
        name="proj_matmul",
    )(xb, w_all, col_scale)


def _ret_kernel(lg_ref, q_ref, k_ref, v_ref, g_ref, gn_ref, o_ref,
                state_ref, inner_ref, qd_ref, kd_ref, cd_ref):
    hg = pl.program_id(1)
    nh = RET_HEADS_PER_STEP
    seq = q_ref.shape[0]
    c = CHUNK
    row = lax.broadcasted_iota(jnp.int32, (c, c), 0)
    col = lax.broadcasted_iota(jnp.int32, (c, c), 1)
    rel = (row - col).astype(F32)
    causal = row >= col
    rowf = row.astype(F32)
    for i in range(nh):
        lg = lg_ref[hg * nh + i]
        inner_ref[i] = jnp.where(causal, jnp.exp(lg * jnp.where(causal, rel, 0.0)), 0.0)
        qd_ref[i] = jnp.exp(lg * (rowf + 1.0))
        kd_ref[i] = jnp.exp(lg * (c - 1.0 - col.astype(F32)))
        cd_ref[i] = jnp.exp(lg * (c + 0.0 * rowf))
    state_ref[...] = jnp.zeros_like(state_ref)

    def body(ci, carry):
        r0 = pl.multiple_of(ci * c, c)
        for i in range(nh):
            lanes = slice(i * HEAD_DIM, (i + 1) * HEAD_DIM)
            q = q_ref[pl.ds(r0, c), lanes]
            k = k_ref[pl.ds(r0, c), lanes]
            v = v_ref[pl.ds(r0, c), lanes]
            g = g_ref[pl.ds(r0, c), lanes].astype(F32)
            state = state_ref[i]
            kt = k.astype(F32).T
            qk_qs = _dot(q, jnp.concatenate([kt.astype(BF16), state.astype(BF16)], axis=1))
            scores = (qk_qs[:, :c] * inner_ref[i]).astype(BF16)
            kd_t = (kt * kd_ref[i]).astype(BF16)
            sv_kv = _dot(jnp.concatenate([scores, kd_t], axis=0), v)
            o = sv_kv[:c] + qk_qs[:, c:] * qd_ref[i]
            state_ref[i] = state * cd_ref[i] + sv_kv[c:]
            mu = jnp.mean(o, axis=-1, keepdims=True)
            oc = o - mu
            var = jnp.mean(oc * oc, axis=-1, keepdims=True)
            y = oc * lax.rsqrt(var + LN_EPS) * gn_ref[:, lanes] * _silu(g)
            o_ref[pl.ds(r0, c), lanes] = y.astype(o_ref.dtype)
        return carry

    lax.fori_loop(0, seq // c, body, 0, unroll=4)


def _retention(proj, gn_g, log_gamma, batch, seq):
    nh = RET_HEADS_PER_STEP
    w = nh * HEAD_DIM
    blk = lambda base: pl.BlockSpec((seq, w), lambda b, h: (b, base // nh + h))
    tab = pltpu.VMEM((nh, CHUNK, CHUNK), F32)
    return pl.pallas_call(
        _ret_kernel,
        grid=(batch, N_RET // nh),
        in_specs=[pl.BlockSpec(memory_space=pltpu.SMEM),
                  blk(RET_Q), blk(RET_K), blk(RET_V), blk(RET_G),
                  pl.BlockSpec((1, w), lambda b, h: (0, h))],
        out_specs=pl.BlockSpec((seq, w), lambda b, h: (b, h)),
        out_shape=jax.ShapeDtypeStruct((batch * seq, RET_W), BF16),
        scratch_shapes=[pltpu.VMEM((nh, HEAD_DIM, HEAD_DIM), F32), tab, tab, tab, tab],
        compiler_params=_params("parallel", "parallel"),
        name="retention",
    )(log_gamma, proj, proj, proj, proj, gn_g)


def _sb_parts(q_ref, k_ref, v_ref, g_ref, o_ref, acc_ref, car_ref):
    tq, tk = ATT_TQ, ATT_TK
    sub = tq // tk
    row = lax.broadcasted_iota(jnp.int32, (tk, tk), 0)
    col = lax.broadcasted_iota(jnp.int32, (tk, tk), 1)
    upper = jnp.where(row > col, 1.0, 0.0).astype(BF16)
    row_in_blk = lax.broadcasted_iota(jnp.int32, (tq, tk), 0) & (tk - 1)
    strict = lax.broadcasted_iota(jnp.int32, (tq, tk), 1) < row_in_blk

    def stage1(z, mask):
        rows = z.shape[0]
        e = jnp.exp2(_neg_abs(z))
        nl = jnp.maximum(z, 0.0) + jnp.log2(1.0 + e)
        if mask is not None:
            nl = jnp.where(mask, nl, 0.0)
        rsum = jnp.broadcast_to(jnp.sum(nl, axis=-1, keepdims=True), (rows, HEAD_DIM))
        return z - nl, nl.astype(BF16), rsum

    def stage2(zl, hl, rsum, r0, mask, v_blocks):
        rows = zl.shape[0]
        rs = pl.ds(r0, rows)
        car = car_ref[rs, :]
        a = jnp.exp2(zl - (_dot(hl, upper) + _lanes2(car)))
        if mask is not None:
            a = jnp.where(mask, a, 0.0)
        a = a.astype(BF16)
        if len(v_blocks) == 1:
            upd = _dot(a, v_blocks[0])
        else:
            upd = jnp.concatenate(
                [_dot(a[i * tk:(i + 1) * tk], v) for i, v in enumerate(v_blocks)], axis=0)
        acc_ref[rs, :] += upd
        car_ref[rs, :] = car + rsum

    def init(q0):
        acc_ref[...] = jnp.zeros_like(acc_ref)
        car_ref[...] = jnp.zeros_like(car_ref)

    def diag(q0):
        def diag1(d):
            n = sub - d
            q = q_ref[pl.ds(q0 + d * tk, n * tk), :]
            z = jnp.concatenate(
                [_dot_nt(q[i * tk:(i + 1) * tk], k_ref[pl.ds(q0 + i * tk, tk), :]) for i in range(n)],
                axis=0)
            mask = strict if d == 0 else None
            return stage1(z, mask)

        def diag2(d, staged):
            n = sub - d
            zl, hl, rsum = staged
            mask = strict if d == 0 else None
            stage2(zl, hl, rsum, d * tk, mask, [v_ref[pl.ds(q0 + i * tk, tk), :] for i in range(n)])

        staged = diag1(0)
        for d in range(sub):
            nxt = diag1(d + 1) if d + 1 < sub else None
            diag2(d, staged)
            staged = nxt

    def ktile(q0, t):
        for j in range(sub):
            k0 = pl.multiple_of(((t + 1) * sub - 1 - j) * tk, tk)
            z = _dot_nt(q_ref[pl.ds(q0, tq), :], k_ref[pl.ds(k0, tk), :])
            zl, hl, rsum = stage1(z, None)
            stage2(zl, hl, rsum, 0, None, [v_ref[pl.ds(k0, tk), :]])

    def final(q0):
        g = g_ref[pl.ds(q0, tq), :].astype(F32)
        o_ref[pl.ds(q0, tq), :] = (acc_ref[...] * _silu(g)).astype(o_ref.dtype)

    return init, diag, ktile, final


def _diff_parts(slope_ref, lamv_ref, q_ref, k_ref, v_ref, g_ref, gain_ref, o_ref,
                q1_ref, q2_ref, accl1_ref, accl2_ref, m1_ref, m2_ref, *, out_scale, lambda_init):
    h = pl.program_id(1)
    sl2 = slope_ref[h] * LOG2E
    tq, tk = ATT_TQ, ATT_TK
    sub = tq // tk
    half = HEAD_DIM // 2
    causal = (lax.broadcasted_iota(jnp.int32, (tk, tk), 1)
              <= lax.broadcasted_iota(jnp.int32, (tk, tk), 0))
    ones = jnp.ones((tk, HEAD_DIM), BF16)
    frame_shift = sl2 * tk
    lane_q = lax.broadcasted_iota(jnp.int32, (tq, HEAD_DIM), 1)
    lane_k = lax.broadcasted_iota(jnp.int32, (tk, HEAD_DIM), 1)
    s_full = jnp.full((tq, HEAD_DIM), sl2, F32)
    s_hi = s_full.astype(BF16).astype(F32)
    s_mid = (s_full - s_hi).astype(BF16).astype(F32)
    s_lo = (s_full - s_hi - s_mid).astype(BF16).astype(F32)

    def slope_feats(base):
        f = jnp.where(lane_q == base, s_hi,
                      jnp.where(lane_q == base + 1, s_mid, jnp.where(lane_q == base + 2, s_lo, 0.0)))
        return f.astype(BF16)

    key_pos = lax.broadcasted_iota(jnp.int32, (tk, HEAD_DIM), 0).astype(F32)

    def pos_feats(base):
        f = jnp.where(lane_k >= base, jnp.where(lane_k < base + 3, key_pos, 0.0), 0.0)
        return f.astype(BF16)

    def keep(lane, first):
        return jnp.where((lane < half) if first else (lane >= half), 1.0, 0.0).astype(BF16)

    q_feats = (slope_feats(half), slope_feats(0))
    k_feats = (pos_feats(half), pos_feats(0))
    q_keep = (keep(lane_q, True), keep(lane_q, False))
    k_keep = (keep(lane_k, True), keep(lane_k, False))

    lv = lamv_ref[...]
    lam = (jnp.exp(jnp.sum(lv[0:1] * lv[1:2], axis=-1, keepdims=True))
           - jnp.exp(jnp.sum(lv[2:3] * lv[3:4], axis=-1, keepdims=True)) + lambda_init)

    maps = ((q1_ref, m1_ref, accl1_ref), (q2_ref, m2_ref, accl2_ref))

    def block(r0, rows, k_starts, mask):
        rs = pl.ds(r0, rows)
        vs = [jnp.concatenate([v_ref[pl.ds(k0, tk), :], ones], axis=1) for k0 in k_starts]
        for mi, (qm_ref, m_ref, accl_ref) in enumerate(maps):
            parts = []
            for i, k0 in enumerate(k_starts):
                k = k_ref[pl.ds(k0, tk), :] * k_keep[mi] + k_feats[mi]
                rows_i = rows if len(k_starts) == 1 else tk
                parts.append(_dot_nt(qm_ref[pl.ds(r0 + i * tk, rows_i), :], k))
            y = parts[0] if len(parts) == 1 else jnp.concatenate(parts, axis=0)
            if mask is not None:
                y = jnp.where(mask, y, NEG)
            m_blk = jnp.broadcast_to(jnp.max(y, axis=-1, keepdims=True), (rows, HEAD_DIM))
            m_old = m_ref[rs, :] - frame_shift
            m_new = jnp.maximum(m_old, m_blk)
            p = jnp.exp2(y - _lanes2(m_new)).astype(BF16)
            alpha = jnp.exp2(m_old - m_new)
            if len(vs) == 1:
                upd = _dot(p, vs[0])
            else:
                upd = jnp.concatenate(
                    [_dot(p[i * tk:(i + 1) * tk], v) for i, v in enumerate(vs)], axis=0)
            accl_ref[rs, :] = _lanes2(alpha) * accl_ref[rs, :] + upd
            m_ref[rs, :] = m_new

    def init(q0):
        q = q_ref[pl.ds(q0, tq), :]
        q1_ref[...] = q * q_keep[0] + q_feats[0]
        q2_ref[...] = q * q_keep[1] + q_feats[1]
        for _, m_ref, accl_ref in maps:
            m_ref[...] = jnp.full_like(m_ref, NEG)
            accl_ref[...] = jnp.zeros_like(accl_ref)

    def ktile(q0, t):
        for j in range(sub):
            block(0, tq, [pl.multiple_of((sub * t + j) * tk, tk)], None)

    def diag(q0):
        for jd in range(sub):
            k0 = q0 + jd * tk
            block(jd * tk, tk, [k0], causal)
            if jd + 1 < sub:
                block((jd + 1) * tk, tq - (jd + 1) * tk, [k0], None)

    def final(q0):
        o1 = accl1_ref[:, :HEAD_DIM] * (1.0 / accl1_ref[:, HEAD_DIM:])
        o2 = accl2_ref[:, :HEAD_DIM] * (1.0 / accl2_ref[:, HEAD_DIM:])
        o = o1 - lam * o2
        o = o * lax.rsqrt(jnp.mean(o * o, axis=-1, keepdims=True) + LN_EPS)
        g = g_ref[pl.ds(q0, tq), :].astype(F32)
        o_ref[pl.ds(q0, tq), :] = (o * (gain_ref[...] * out_scale) * _silu(g)).astype(o_ref.dtype)

    return init, diag, ktile, final


def _attn_kernel(slope_ref, lamv_ref, sq_ref, sk_ref, sv_ref, sg_ref, dq_ref, dk_ref, dv_ref, dg_ref,
                 gain_ref, so_ref, do_ref, acc_ref, car_ref,
                 q1_ref, q2_ref, accl1_ref, accl2_ref, m1_ref, m2_ref, *, out_scale, lambda_init):
    seq = sq_ref.shape[0]
    tq = ATT_TQ
    sb_init, sb_diag, sb_ktile, sb_final = _sb_parts(sq_ref, sk_ref, sv_ref, sg_ref, so_ref, acc_ref, car_ref)
    df_init, df_diag, df_ktile, df_final = _diff_parts(
        slope_ref, lamv_ref, dq_ref, dk_ref, dv_ref, dg_ref, gain_ref, do_ref,
        q1_ref, q2_ref, accl1_ref, accl2_ref, m1_ref, m2_ref,
        out_scale=out_scale, lambda_init=lambda_init)
    for qi in range(seq // tq):
        q0 = qi * tq
        sb_init(q0)
        df_init(q0)
        sb_diag(q0)

        def trip(i, c, q0=q0, qi=qi):
            sb_ktile(q0, qi - 1 - i)
            df_ktile(q0, i)
            return c

        lax.fori_loop(0, qi, trip, 0)
        df_diag(q0)
        sb_final(q0)
        df_final(q0)


def _attention(proj, gain, lamv, slopes, batch, seq, *, lambda_init):
    assert N_SB == N_DIFF
    blk = lambda base: pl.BlockSpec((seq, HEAD_DIM), lambda b, h: (b, base + h))
    out_blk = pl.BlockSpec((seq, HEAD_DIM), lambda b, h: (b, h))
    rows128 = pltpu.VMEM((ATT_TQ, HEAD_DIM), F32)
    qm = pltpu.VMEM((ATT_TQ, HEAD_DIM), BF16)
    accl = pltpu.VMEM((ATT_TQ, 2 * HEAD_DIM), F32)
    return pl.pallas_call(
        functools.partial(_attn_kernel, out_scale=1.0 - lambda_init, lambda_init=lambda_init),
        grid=(batch, N_SB),
        in_specs=[pl.BlockSpec(memory_space=pltpu.SMEM),
                  pl.BlockSpec((4, HEAD_DIM // 2), lambda b, h: (0, 0)),
                  blk(SB_Q), blk(SB_K), blk(SB_V), blk(SB_G),
                  blk(DIFF_Q), blk(DIFF_K), blk(DIFF_V), blk(DIFF_G),
                  pl.BlockSpec((1, HEAD_DIM), lambda b, h: (0, h))],
        out_specs=[out_blk, out_blk],
        out_shape=[jax.ShapeDtypeStruct((batch * seq, SB_W), BF16),
                   jax.ShapeDtypeStruct((batch * seq, DIFF_W), BF16)],
        scratch_shapes=[rows128, rows128, qm, qm, accl, accl, rows128, rows128],
        compiler_params=_params("parallel", "parallel"),
        name="sb_diff_attention",
    )(slopes, lamv, proj, proj, proj, proj, proj, proj, proj, proj, gain)


def _outproj_kernel(ro_ref, so_ref, do_ref, w_ref, z_ref, mu_ref, rs_ref, g_ref, b_ref, o_ref, *, alpha):
    y = _dot(ro_ref[...], w_ref[0:RET_W, :])
    y = y + _dot(so_ref[...], w_ref[RET_W:RET_W + SB_W, :])
    y = y + _dot(do_ref[...], w_ref[RET_W + SB_W:, :])
    reps = z_ref.shape[1] // HEAD_DIM
    mu = jnp.concatenate([mu_ref[...]] * reps, axis=1)
    rs = jnp.concatenate([rs_ref[...]] * reps, axis=1)
    x = (z_ref[...] - mu) * rs * g_ref[...] + b_ref[...]
    o_ref[...] = alpha * x + y


def _outproj(ro, so, do, w_all, layer, z_prev, mu, rs, g, b, *, alpha, tm=1024, tn=1024):
    m = ro.shape[0]
    k, n = w_all.shape[1], w_all.shape[2]
    row_blk = lambda w: pl.BlockSpec((tm, w), lambda i, j: (i, 0))
    col_vec = pl.BlockSpec((1, tn), lambda i, j: (0, j))
    return pl.pallas_call(
        functools.partial(_outproj_kernel, alpha=alpha),
        grid=(m // tm, n // tn),
        in_specs=[row_blk(RET_W), row_blk(SB_W), row_blk(DIFF_W),
                  pl.BlockSpec((None, k, tn), lambda i, j: (layer, 0, j)),
                  pl.BlockSpec((tm, tn), lambda i, j: (i, j)),
                  row_blk(HEAD_DIM), row_blk(HEAD_DIM), col_vec, col_vec],
        out_specs=pl.BlockSpec((tm, tn), lambda i, j: (i, j)),
        out_shape=jax.ShapeDtypeStruct((m, n), F32),
        compiler_params=_params("parallel", "parallel"),
        name="out_proj_residual",
    )(ro, so, do, w_all, z_prev, mu, rs, g, b)


def _ln_rows(z_ref):
    z = z_ref[...]
    mu = jnp.mean(z, axis=-1, keepdims=True)
    zc = z - mu
    var = jnp.mean(zc * zc, axis=-1, keepdims=True)
    return zc, mu, lax.rsqrt(var + LN_EPS)


def _ln_stats_kernel(z_ref, g_ref, b_ref, ob_ref, mu_ref, rs_ref):
    zc, mu, rs = _ln_rows(z_ref)
    ob_ref[...] = (zc * rs * g_ref[...] + b_ref[...]).astype(ob_ref.dtype)
    mu_ref[...] = jnp.broadcast_to(mu, mu_ref.shape)
    rs_ref[...] = jnp.broadcast_to(rs, rs_ref.shape)


def _ln_final_kernel(z_ref, g_ref, b_ref, o_ref):
    zc, _, rs = _ln_rows(z_ref)
    o_ref[...] = zc * rs * g_ref[...] + b_ref[...]


def _layer_norm(z, g, b, *, final, tr=256):
    m, d = z.shape
    rows = lambda w: pl.BlockSpec((tr, w), lambda i: (i, 0))
    vec = pl.BlockSpec((1, d), lambda i: (0, 0))
    if final:
        kern, out_specs = _ln_final_kernel, rows(d)
        out_shape = jax.ShapeDtypeStruct((m, d), F32)
    else:
        kern, out_specs = _ln_stats_kernel, [rows(d), rows(HEAD_DIM), rows(HEAD_DIM)]
        out_shape = [jax.ShapeDtypeStruct((m, d), BF16), jax.ShapeDtypeStruct((m, HEAD_DIM), F32),
                     jax.ShapeDtypeStruct((m, HEAD_DIM), F32)]
    return pl.pallas_call(
        kern,
        grid=(m // tr,),
        in_specs=[rows(d), vec, vec],
        out_specs=out_specs,
        out_shape=out_shape,
        compiler_params=_params("parallel"),
        name="layer_norm",
    )(z, g, b)


def kernel(x, w_in, w_out, ln_g, ln_b, ret_gn_g, diff_ln_g, lam_q1, lam_k1, lam_q2, lam_k2):
    batch, seq, d_model = x.shape
    depth = w_in.shape[0]
    alpha = (2.0 * depth) ** 0.25
    log_gamma = jnp.asarray(_retention_log_decay(N_RET))
    slopes = jnp.asarray(_alibi_slopes(N_DIFF))
    col_scale = jnp.asarray(_proj_col_scale())
    w_out_b = w_out.astype(BF16)

    m = batch * seq
    z = x.reshape(m, d_model)
    xb = z.astype(BF16)
    mu = jnp.zeros((m, HEAD_DIM), F32)
    rs = jnp.ones((m, HEAD_DIM), F32)
    g = jnp.ones((1, d_model), F32)
    b = jnp.zeros((1, d_model), F32)
    for layer in range(depth):
        lambda_init = 0.8 - 0.6 * math.exp(-0.3 * layer)
        proj = _proj(xb, w_in, layer, col_scale)
        ro = _retention(proj, ret_gn_g[layer][None, :], log_gamma, batch, seq)
        lamv = jnp.stack([lam_q1[layer], lam_k1[layer], lam_q2[layer], lam_k2[layer]]).astype(F32)
        so, do = _attention(proj, diff_ln_g[layer][None, :], lamv, slopes, batch, seq,
                            lambda_init=lambda_init)
        z = _outproj(ro, so, do, w_out_b, layer, z, mu, rs, g, b, alpha=alpha)
        g, b = ln_g[layer][None, :], ln_b[layer][None, :]
        if layer + 1 < depth:
            xb, mu, rs = _layer_norm(z, g, b, final=False)
    return _layer_norm(z, g, b, final=True).reshape(batch, seq, d_model)
```
